```python
import math
import jax, jax.numpy as jnp
from jax import lax
import numpy as np

D_MODEL = 1024
BATCH = 8
SEQ = 2048
DEPTH = 1

N_MEM = 256
RMS_EPS = 1e-6
RG_HEADS = 8
RG_HEAD_DIM = 64
RG_W = RG_HEADS * RG_HEAD_DIM
CONV_WIDTH = 4
LRU_C = 8.0
DA_HEADS = 4
DA_HEAD_DIM = 64
DA_V_DIM = 2 * DA_HEAD_DIM
DA_W = DA_HEADS * DA_V_DIM
LAYER_INDEX = 1
LAMBDA_INIT = 0.8 - 0.6 * math.exp(-0.3 * (LAYER_INDEX - 1))
Q_BLOCK = 128
XA_HEADS = 4
XA_HEAD_DIM = 128
XA_W = XA_HEADS * XA_HEAD_DIM
D_MIX = RG_W + DA_W + XA_W
IN_SPLITS = [RG_W, RG_W,
             DA_HEADS * 2 * DA_HEAD_DIM, DA_HEADS * 2 * DA_HEAD_DIM,
             DA_W, DA_W,
             XA_W, XA_W]
D_IN = sum(IN_SPLITS)

kernel_name = "hymba_style_rglru_diffattn_memxattn_layer"


def rms_norm(x, g, eps=RMS_EPS):
    xf = x.astype(jnp.float32)
    y = xf * lax.rsqrt(jnp.mean(xf * xf, axis=-1, keepdims=True) + eps)
    return (y * g.astype(jnp.float32)).astype(x.dtype)


def causal_depthwise_conv(x, w, b):
    s = x.shape[1]
    xp = jnp.pad(x, ((0, 0), (CONV_WIDTH - 1, 0), (0, 0)))
    y = b
    for k in range(CONV_WIDTH):
        y = y + w[k] * xp[:, k:k + s]
    return y


def rg_lru(xc, w_rg_a, b_rg_a, w_rg_x, b_rg_x, lru_lambda):
    b_, s, _ = xc.shape
    xh = xc.reshape(b_, s, RG_HEADS, RG_HEAD_DIM)
    r = jax.nn.sigmoid(jnp.einsum('bshi,hij->bshj', xh, w_rg_a)
                       + b_rg_a.reshape(RG_HEADS, RG_HEAD_DIM))
    i = jax.nn.sigmoid(jnp.einsum('bshi,hij->bshj', xh, w_rg_x)
                       + b_rg_x.reshape(RG_HEADS, RG_HEAD_DIM))
    r = r.reshape(b_, s, RG_W).astype(jnp.float32)
    i = i.reshape(b_, s, RG_W).astype(jnp.float32)
    log_a = -LRU_C * r * jax.nn.softplus(-lru_lambda.astype(jnp.float32))
    a = jnp.exp(log_a)
    mult = jnp.sqrt(jnp.maximum(1.0 - jnp.exp(2.0 * log_a), 0.0))
    u = mult * (i * xc.astype(jnp.float32))

    def combine(c1, c2):
        a1, b1 = c1
        a2, b2 = c2
        return a1 * a2, a2 * b1 + b2

    _, h = lax.associative_scan(combine, (a, u), axis=1)
    return h.astype(xc.dtype)


def diff_attention(q, k, v, lam):
    s = q.shape[1]
    scale = DA_HEAD_DIM ** -0.5
    q1, q2 = q[..., 0, :], q[..., 1, :]
    k1, k2 = k[..., 0, :], k[..., 1, :]
    outs = []
    for blk in range(s // Q_BLOCK):
        q0, kend = blk * Q_BLOCK, (blk + 1) * Q_BLOCK
        mask = (q0 + jnp.arange(Q_BLOCK))[:, None] >= jnp.arange(kend)[None, :]
        s1 = jnp.einsum('bqhd,bkhd->bhqk', q1[:, q0:kend], k1[:, :kend]).astype(jnp.float32) * scale
        s2 = jnp.einsum('bqhd,bkhd->bhqk', q2[:, q0:kend], k2[:, :kend]).astype(jnp.float32) * scale
        p1 = jax.nn.softmax(jnp.where(mask, s1, -jnp.inf), axis=-1)
        p2 = jax.nn.softmax(jnp.where(mask, s2, -jnp.inf), axis=-1)
        p = (p1 - lam * p2).astype(v.dtype)
        outs.append(jnp.einsum('bhqk,bkhe->bqhe', p, v[:, :kend]))
    return jnp.concatenate(outs, axis=1)


def mem_cross_attention(q, km, vm):
    sc = jnp.einsum('bshd,bmhd->bhsm', q, km).astype(jnp.float32) * (XA_HEAD_DIM ** -0.5)
    p = jax.nn.softmax(sc, axis=-1).astype(vm.dtype)
    return jnp.einsum('bhsm,bmhd->bshd', p, vm)


def setup_inputs(seed: int = 0) -> dict:
    key = jax.random.key(seed)
    ks = jax.random.split(key, 24)
    f32 = jnp.float32
    nrm = lambda k, shp, sc: jax.random.normal(k, shp, f32) * sc
    a0 = jax.random.uniform(ks[10], (RG_W,), f32, 0.9, 0.999)
    base = a0 ** (1.0 / LRU_C)
    lru_lambda = jnp.log(base / (1.0 - base))
    return {
        "x": nrm(ks[0], (BATCH, SEQ, D_MODEL), 1.0),
        "mem": nrm(ks[1], (BATCH, N_MEM, D_MODEL), 1.0),
        "g_pre": 1.0 + nrm(ks[2], (D_MODEL,), 0.02),
        "g_mem": 1.0 + nrm(ks[3], (D_MODEL,), 0.02),
        "w_in": nrm(ks[4], (D_MODEL, D_IN), D_MODEL ** -0.5),
        "w_mem_kv": nrm(ks[5], (D_MODEL, 2 * XA_W), D_MODEL ** -0.5),
        "conv_w": nrm(ks[6], (CONV_WIDTH, RG_W), CONV_WIDTH ** -0.5),
        "conv_b": nrm(ks[7], (RG_W,), 0.01),
        "w_rg_a": nrm(ks[8], (RG_HEADS, RG_HEAD_DIM, RG_HEAD_DIM), RG_HEAD_DIM ** -0.5),
        "b_rg_a": nrm(ks[9], (RG_W,), 0.01),
        "w_rg_x": nrm(ks[11], (RG_HEADS, RG_HEAD_DIM, RG_HEAD_DIM), RG_HEAD_DIM ** -0.5),
        "b_rg_x": nrm(ks[12], (RG_W,), 0.01),
        "lru_lambda": lru_lambda,
        "lambda_q1": nrm(ks[13], (DA_HEAD_DIM,), 0.1),
        "lambda_k1": nrm(ks[14], (DA_HEAD_DIM,), 0.1),
        "lambda_q2": nrm(ks[15], (DA_HEAD_DIM,), 0.1),
        "lambda_k2": nrm(ks[16], (DA_HEAD_DIM,), 0.1),
        "g_subln": 1.0 + nrm(ks[17], (DA_V_DIM,), 0.02),
        "w_out": nrm(ks[18], (D_MIX, D_MODEL), D_MIX ** -0.5),
        "g_post": 1.0 + nrm(ks[19], (D_MODEL,), 0.02),
    }


def reference(x, mem, g_pre, g_mem, w_in, w_mem_kv, conv_w, conv_b, w_rg_a, b_rg_a,
              w_rg_x, b_rg_x, lru_lambda, lambda_q1, lambda_k1, lambda_q2, lambda_k2,
              g_subln, w_out, g_post):
    b_, s, _ = x.shape
    memn = rms_norm(mem, g_mem)
    km, vm = jnp.split(memn @ w_mem_kv, 2, axis=-1)
    km = km.reshape(b_, N_MEM, XA_HEADS, XA_HEAD_DIM)
    vm = vm.reshape(b_, N_MEM, XA_HEADS, XA_HEAD_DIM)
    lam = (jnp.exp(jnp.sum(lambda_q1.astype(jnp.float32) * lambda_k1.astype(jnp.float32)))
           - jnp.exp(jnp.sum(lambda_q2.astype(jnp.float32) * lambda_k2.astype(jnp.float32)))
           + LAMBDA_INIT)

    for _ in range(DEPTH):
        hn = rms_norm(x, g_pre)
        proj = hn @ w_in
        cuts = np.cumsum(IN_SPLITS)[:-1].tolist()
        rg_x, rg_g, da_q, da_k, da_v, da_g, xa_q, xa_g = jnp.split(proj, cuts, axis=-1)

        xc = causal_depthwise_conv(rg_x, conv_w, conv_b)
        y_rg = rg_lru(xc, w_rg_a, b_rg_a, w_rg_x, b_rg_x, lru_lambda) * jax.nn.silu(rg_g)

        q = da_q.reshape(b_, s, DA_HEADS, 2, DA_HEAD_DIM)
        k = da_k.reshape(b_, s, DA_HEADS, 2, DA_HEAD_DIM)
        v = da_v.reshape(b_, s, DA_HEADS, DA_V_DIM)
        o = diff_attention(q, k, v, lam)
        o = rms_norm(o, g_subln) * (1.0 - LAMBDA_INIT)
        y_da = o.reshape(b_, s, DA_W) * jax.nn.silu(da_g)

        oc = mem_cross_attention(xa_q.reshape(b_, s, XA_HEADS, XA_HEAD_DIM), km, vm)
        y_xa = oc.reshape(b_, s, XA_W) * jax.nn.silu(xa_g)

        y = jnp.concatenate([y_rg, y_da, y_xa], axis=-1) @ w_out
        x = x + rms_norm(y, g_post)
    return x
```

```python
import functools
import math

import jax
import jax.numpy as jnp
from jax import lax
from jax.experimental import pallas as pl
from jax.experimental.pallas import tpu as pltpu

F32 = jnp.float32
BF16 = jnp.bfloat16

RMS_EPS = 1e-6
N_MEM = 256
RG_HEADS = 8
RG_HEAD_DIM = 64
RG_W = RG_HEADS * RG_HEAD_DIM
CONV_WIDTH = 4
LRU_C = 8.0
DA_HEADS = 4
DA_HEAD_DIM = 64
DA_V_DIM = 2 * DA_HEAD_DIM
DA_W = DA_HEADS * DA_V_DIM
LAMBDA_INIT = 0.8 - 0.6 * math.exp(-0.3 * (1 - 1))
XA_HEADS = 4
XA_HEAD_DIM = 128
XA_W = XA_HEADS * XA_HEAD_DIM
REST_W = 6 * 512
LOG2E = 1.4426950408889634

VMEM_LIMIT_BYTES = 56 * 1024 * 1024


def _params(sem):
    return pltpu.CompilerParams(dimension_semantics=sem, vmem_limit_bytes=VMEM_LIMIT_BYTES)


def _rms(x, g):
    ms = jnp.mean(x * x, axis=-1, keepdims=True)
    return x * lax.rsqrt(ms + RMS_EPS) * g


def _silu(x):
    return x * jax.nn.sigmoid(x)


def _memkv_kernel(mem_ref, g_ref, w_ref, out_ref):
    mn = _rms(mem_ref[...], g_ref[...]).astype(BF16)
    out_ref[...] = jnp.dot(mn, w_ref[...], preferred_element_type=F32).astype(BF16)


def _memkv(mem, g_mem, w_bf):
    b, m, d = mem.shape
    return pl.pallas_call(
        _memkv_kernel,
        grid=(b,),
        in_specs=[
            pl.BlockSpec((None, m, d), lambda i: (i, 0, 0)),
            pl.BlockSpec((1, d), lambda i: (0, 0)),
            pl.BlockSpec((d, 2 * XA_W), lambda i: (0, 0)),
        ],
        out_specs=pl.BlockSpec((None, m, 2 * XA_W), lambda i: (i, 0, 0)),
        out_shape=jax.ShapeDtypeStruct((b, m, 2 * XA_W), BF16),
        compiler_params=_params(("arbitrary",)),
        name="memkv",
    )(mem, g_mem.reshape(1, d), w_bf)


def _inproj_kernel(x_ref, g_ref, w_ref, rgx_ref, rgg_ref, rest_ref):
    hn = _rms(x_ref[...], g_ref[...]).astype(BF16)
    rgx_ref[...] = jnp.dot(hn, w_ref[:, 0:RG_W], preferred_element_type=F32)
    rgg_ref[...] = jnp.dot(hn, w_ref[:, RG_W:2 * RG_W], preferred_element_type=F32)
    for n in range(REST_W // 512):
        lo = 2 * RG_W + n * 512
        rest_ref[:, n * 512:(n + 1) * 512] = jnp.dot(
            hn, w_ref[:, lo:lo + 512], preferred_element_type=F32).astype(BF16)


def _inproj(x, g_pre, w_bf, tm):
    b, s, d = x.shape
    d_in = w_bf.shape[1]
    return pl.pallas_call(
        _inproj_kernel,
        grid=(b, s // tm),
        in_specs=[
            pl.BlockSpec((None, tm, d), lambda i, j: (i, j, 0)),
            pl.BlockSpec((1, d), lambda i, j: (0, 0)),
            pl.BlockSpec((d, d_in), lambda i, j: (0, 0)),
        ],
        out_specs=[
            pl.BlockSpec((tm, RG_W), lambda i, j: (j, i)),
            pl.BlockSpec((tm, RG_W), lambda i, j: (j, i)),
            pl.BlockSpec((None, tm, REST_W), lambda i, j: (i, j, 0)),
        ],
        out_shape=[
            jax.ShapeDtypeStruct((s, b * RG_W), F32),
            jax.ShapeDtypeStruct((s, b * RG_W), F32),
            jax.ShapeDtypeStruct((b, s, REST_W), BF16),
        ],
        compiler_params=_params(("arbitrary", "arbitrary")),
        name="inproj",
    )(x, g_pre.reshape(1, d), w_bf)


def _rglru_kernel(x_ref, g_ref, cw_ref, cb_ref, wa_ref, ba_ref, wx_ref, bx_ref, lam_ref,
                  y_ref, xp_ref, a_ref, u_ref, h_ref, *, ts, nb):
    step = pl.program_id(0)
    halo = CONV_WIDTH - 1

    @pl.when(step == 0)
    def _():
        xp_ref[0:halo] = jnp.zeros((halo, nb, RG_W), F32)
        h_ref[...] = jnp.zeros((nb, RG_W), F32)

    @pl.when(step > 0)
    def _():
        xp_ref[0:halo] = xp_ref[ts:ts + halo]

    xp_ref[halo:halo + ts] = x_ref[...]

    xc = cb_ref[...].reshape(1, 1, RG_W)
    for k in range(CONV_WIDTH):
        xc = xc + cw_ref[k:k + 1, :].reshape(1, 1, RG_W) * xp_ref[k:k + ts]

    xb = xc.reshape(ts * nb, RG_W).astype(BF16)
    r = jax.nn.sigmoid(jnp.dot(xb, wa_ref[...], preferred_element_type=F32) + ba_ref[...])
    i = jax.nn.sigmoid(jnp.dot(xb, wx_ref[...], preferred_element_type=F32) + bx_ref[...])
    r = r.reshape(ts, nb, RG_W)
    i = i.reshape(ts, nb, RG_W)

    z = -lam_ref[...]
    softplus = jnp.maximum(z, 0.0) + jnp.log1p(jnp.exp(-jnp.abs(z)))
    log_a = (-LRU_C * r) * softplus.reshape(1, 1, RG_W)
    a = jnp.exp(log_a)
    mult = jnp.sqrt(jnp.maximum(1.0 - a * a, 0.0))
    a_ref[...] = a
    u_ref[...] = mult * (i * xc)

    def body(t, h):
        h = a_ref[t] * h + u_ref[t]
        u_ref[t] = h
        return h

    h_ref[...] = lax.fori_loop(0, ts, body, h_ref[...], unroll=8)
    y_ref[...] = u_ref[...] * _silu(g_ref[...])


def _block_diag(w):
    h, di, dj = w.shape
    eye = jnp.eye(h, dtype=w.dtype)
    return (eye[:, None, :, None] * w[:, :, None, :]).reshape(h * di, h * dj)


def _rglru(rgx, rgg, conv_w, conv_b, w_rg_a, b_rg_a, w_rg_x, b_rg_x, lru_lambda, nb, ts):
    s = rgx.shape[0]
    x3 = rgx.reshape(s, nb, RG_W)
    g3 = rgg.reshape(s, nb, RG_W)
    wa = _block_diag(w_rg_a).astype(BF16)
    wx = _block_diag(w_rg_x).astype(BF16)
    row = lambda v: v.reshape(1, RG_W)
    const2 = lambda shape: pl.BlockSpec(shape, lambda i: (0, 0))
    tile = pl.BlockSpec((ts, nb, RG_W), lambda i: (i, 0, 0))
    y3 = pl.pallas_call(
        functools.partial(_rglru_kernel, ts=ts, nb=nb),
        grid=(s // ts,),
        in_specs=[tile, tile, const2((CONV_WIDTH, RG_W)), const2((1, RG_W)),
                  const2((RG_W, RG_W)), const2((1, RG_W)), const2((RG_W, RG_W)), const2((1, RG_W)),
                  const2((1, RG_W))],
        out_specs=tile,
        out_shape=jax.ShapeDtypeStruct((s, nb, RG_W), F32),
        scratch_shapes=[
            pltpu.VMEM((ts + CONV_WIDTH - 1, nb, RG_W), F32),
            pltpu.VMEM((ts, nb, RG_W), F32),
            pltpu.VMEM((ts, nb, RG_W), F32),
            pltpu.VMEM((nb, RG_W), F32),
        ],
        compiler_params=_params(("arbitrary",)),
        name="rglru",
    )(x3, g3, conv_w, row(conv_b), wa, row(b_rg_a), wx, row(b_rg_x), row(lru_lambda))
    return y3.reshape(s, nb * RG_W)


def _nt_dot(a, b):
    return lax.dot_general(a, b, (((1,), (1,)), ((), ())), preferred_element_type=F32)


def _diffattn_kernel(q_ref, k_ref, v_ref, g_ref, lq1_ref, lk1_ref, lq2_ref, lk2_ref, gs_ref,
                     o_ref, *, tq):
    qi = pl.program_id(2)
    lam = (jnp.exp(jnp.sum(lq1_ref[...] * lk1_ref[...], axis=-1, keepdims=True))
           - jnp.exp(jnp.sum(lq2_ref[...] * lk2_ref[...], axis=-1, keepdims=True))
           + LAMBDA_INIT)

    q = q_ref[...]
    lane = lax.broadcasted_iota(jnp.int32, q.shape, 1)
    zero = jnp.zeros_like(q)
    q1 = jnp.where(lane < DA_HEAD_DIM, q, zero)
    q2 = jnp.where(lane >= DA_HEAD_DIM, q, zero)
    c = (DA_HEAD_DIM ** -0.5) * LOG2E

    def update(s, v, m, l, acc):
        m_new = jnp.maximum(m, jnp.max(s, axis=-1, keepdims=True))
        alpha = jnp.exp2(m - m_new)
        p = jnp.exp2(s - m_new)
        l = alpha * l + jnp.sum(p, axis=-1, keepdims=True)
        acc = alpha * acc + jnp.dot(p.astype(BF16), v, preferred_element_type=F32)
        return m_new, l, acc

    def block(j, carry, masked):
        m1, l1, acc1, m2, l2, acc2 = carry
        off = pl.multiple_of(j * tq, tq)
        k = k_ref[pl.ds(off, tq), :]
        v = v_ref[pl.ds(off, tq), :]
        s1 = _nt_dot(q1, k) * c
        s2 = _nt_dot(q2, k) * c
        if masked:
            row = lax.broadcasted_iota(jnp.int32, s1.shape, 0)
            col = lax.broadcasted_iota(jnp.int32, s1.shape, 1)
            keep = row >= col
            s1 = jnp.where(keep, s1, -jnp.inf)
            s2 = jnp.where(keep, s2, -jnp.inf)
        m1, l1, acc1 = update(s1, v, m1, l1, acc1)
        m2, l2, acc2 = update(s2, v, m2, l2, acc2)
        return m1, l1, acc1, m2, l2, acc2

    neg = jnp.full((tq, 1), -jnp.inf, F32)
    zl = jnp.zeros((tq, 1), F32)
    za = jnp.zeros((tq, DA_V_DIM), F32)
    carry = block(qi, (neg, zl, za, neg, zl, za), True)
    carry = lax.fori_loop(0, qi, lambda j, cr: block(j, cr, False), carry)
    m1, l1, acc1, m2, l2, acc2 = carry

    o = acc1 / l1 - lam * (acc2 / l2)
    o = _rms(o, gs_ref[...]) * (1.0 - LAMBDA_INIT)
    o_ref[...] = (o * _silu(g_ref[...].astype(F32))).astype(BF16)


def _diffattn(rest, lq1, lk1, lq2, lk2, g_subln, tq):
    b, s, _ = rest.shape
    hd = 2 * DA_HEAD_DIM
    vec = lambda v: v.reshape(1, -1)
    cvec = lambda n: pl.BlockSpec((1, n), lambda bi, h, i: (0, 0))
    return pl.pallas_call(
        functools.partial(_diffattn_kernel, tq=tq),
        grid=(b, DA_HEADS, s // tq),
        in_specs=[
            pl.BlockSpec((None, tq, hd), lambda bi, h, i: (bi, i, h)),
            pl.BlockSpec((None, s, hd), lambda bi, h, i: (bi, 0, DA_HEADS + h)),
            pl.BlockSpec((None, s, hd), lambda bi, h, i: (bi, 0, 2 * DA_HEADS + h)),
            pl.BlockSpec((None, tq, hd), lambda bi, h, i: (bi, i, 3 * DA_HEADS + h)),
            cvec(DA_HEAD_DIM), cvec(DA_HEAD_DIM), cvec(DA_HEAD_DIM), cvec(DA_HEAD_DIM),
            cvec(DA_V_DIM),
        ],
        out_specs=pl.BlockSpec((None, tq, hd), lambda bi, h, i: (bi, i, h)),
        out_shape=jax.ShapeDtypeStruct((b, s, DA_W), BF16),
        compiler_params=_params(("arbitrary", "arbitrary", "arbitrary")),
        name="diffattn",
    )(rest, rest, rest, rest, vec(lq1), vec(lk1), vec(lq2), vec(lk2), vec(g_subln))


def _memxattn_kernel(q_ref, g_ref, km_ref, vm_ref, o_ref):
    c = (XA_HEAD_DIM ** -0.5) * LOG2E
    for h in range(XA_HEADS):
        sl = slice(h * XA_HEAD_DIM, (h + 1) * XA_HEAD_DIM)
        s = _nt_dot(q_ref[:, sl], km_ref[:, sl]) * c
        m = jnp.max(s, axis=-1, keepdims=True)
        p = jnp.exp2(s - m)
        l = jnp.sum(p, axis=-1, keepdims=True)
        o = jnp.dot(p.astype(BF16), vm_ref[:, sl], preferred_element_type=F32) / l
        o_ref[:, sl] = (o * _silu(g_ref[:, sl].astype(F32))).astype(BF16)


def _memxattn(rest, kv, tq):
    b, s, _ = rest.shape
    m = kv.shape[1]
    return pl.pallas_call(
        _memxattn_kernel,
        grid=(b, s // tq),
        in_specs=[
            pl.BlockSpec((None, tq, XA_W), lambda bi, i: (bi, i, 4)),
            pl.BlockSpec((None, tq, XA_W), lambda bi, i: (bi, i, 5)),
            pl.BlockSpec((None, m, XA_W), lambda bi, i: (bi, 0, 0)),
            pl.BlockSpec((None, m, XA_W), lambda bi, i: (bi, 0, 1)),
        ],
        out_specs=pl.BlockSpec((None, tq, XA_W), lambda bi, i: (bi, i, 0)),
        out_shape=jax.ShapeDtypeStruct((b, s, XA_W), BF16),
        compiler_params=_params(("arbitrary", "arbitrary")),
        name="memxattn",
    )(rest, rest, kv, kv)


def _outproj_kernel(yrg_ref, yda_ref, yxa_ref, w_ref, x_ref, g_ref, o_ref):
    y = jnp.dot(yrg_ref[...].astype(BF16), w_ref[0:RG_W, :], preferred_element_type=F32)
    y = y + jnp.dot(yda_ref[...], w_ref[RG_W:RG_W + DA_W, :], preferred_element_type=F32)
    y = y + jnp.dot(yxa_ref[...], w_ref[RG_W + DA_W:, :], preferred_element_type=F32)
    o_ref[...] = x_ref[...] + _rms(y, g_ref[...])


def _outproj(y_rg, y_da, y_xa, w_bf, x, g_post, tm):
    b, s, d = x.shape
    d_mix = w_bf.shape[0]
    return pl.pallas_call(
        _outproj_kernel,
        grid=(b, s // tm),
        in_specs=[
            pl.BlockSpec((tm, RG_W), lambda i, j: (j, i)),
            pl.BlockSpec((None, tm, DA_W), lambda i, j: (i, j, 0)),
            pl.BlockSpec((None, tm, XA_W), lambda i, j: (i, j, 0)),
            pl.BlockSpec((d_mix, d), lambda i, j: (0, 0)),
            pl.BlockSpec((None, tm, d), lambda i, j: (i, j, 0)),
            pl.BlockSpec((1, d), lambda i, j: (0, 0)),
        ],
        out_specs=pl.BlockSpec((None, tm, d), lambda i, j: (i, j, 0)),
        out_shape=jax.ShapeDtypeStruct((b, s, d), F32),
        compiler_params=_params(("arbitrary", "arbitrary")),
        name="outproj",
    )(y_rg, y_da, y_xa, w_bf, x, g_post.reshape(1, d))


def kernel(x, mem, g_pre, g_mem, w_in, w_mem_kv, conv_w, conv_b, w_rg_a, b_rg_a, w_rg_x, b_rg_x,
           lru_lambda, lambda_q1, lambda_k1, lambda_q2, lambda_k2, g_subln, w_out, g_post):
    b = x.shape[0]
    kv = _memkv(mem, g_mem, w_mem_kv.astype(BF16))
    rgx, rgg, rest = _inproj(x, g_pre, w_in.astype(BF16), tm=512)
    y_rg = _rglru(rgx, rgg, conv_w, conv_b, w_rg_a, b_rg_a, w_rg_x, b_rg_x, lru_lambda, nb=b, ts=128)
    y_da = _diffattn(rest, lambda_q1, lambda_k1, lambda_q2, lambda_k2, g_subln, tq=256)
    y_xa = _memxattn(rest, kv, tq=512)
    return _outproj(y_rg, y_da, y_xa, w_out.astype(BF16), x, g_post, tm=512)
```

```python
import functools
import math

import jax
import jax.numpy as jnp
from jax import lax
from jax.experimental import pallas as pl
from jax.experimental.pallas import tpu as pltpu

F32 = jnp.float32
BF16 = jnp.bfloat16

RMS_EPS = 1e-6
N_MEM = 256
RG_HEADS = 8
RG_HEAD_DIM = 64
RG_W = RG_HEADS * RG_HEAD_DIM
CONV_WIDTH = 4
LRU_C = 8.0
DA_HEADS = 4
DA_HEAD_DIM = 64
DA_V_DIM = 2 * DA_HEAD_DIM
DA_W = DA_HEADS * DA_V_DIM
LAMBDA_INIT = 0.8 - 0.6 * math.exp(-0.3 * (1 - 1))
XA_HEADS = 4
XA_HEAD_DIM = 128
XA_W = XA_HEADS * XA_HEAD_DIM
REST_W = 6 * 512
LOG2E = 1.4426950408889634

SUBLANES = 8
BF16_ROWS = 16
VMEM_LIMIT_BYTES = 56 * 1024 * 1024


def _params(sem):
    return pltpu.CompilerParams(dimension_semantics=sem, vmem_limit_bytes=VMEM_LIMIT_BYTES)


def _rms(x, g):
    ms = jnp.mean(x * x, axis=-1, keepdims=True)
    return x * lax.rsqrt(ms + RMS_EPS) * g


def _silu(x):
    return x * jax.nn.sigmoid(x)


def _nt_dot(a, b):
    return lax.dot_general(a, b, (((1,), (1,)), ((), ())), preferred_element_type=F32)


def _memkv_kernel(mem_ref, g_ref, w_ref, out_ref):
    mn = _rms(mem_ref[...], g_ref[...]).astype(BF16)
    out_ref[...] = jnp.dot(mn, w_ref[...], preferred_element_type=F32).astype(BF16)


def _memkv(mem, g_mem, w_bf):
    b, m, d = mem.shape
    return pl.pallas_call(
        _memkv_kernel,
        grid=(b,),
        in_specs=[
            pl.BlockSpec((None, m, d), lambda i: (i, 0, 0)),
            pl.BlockSpec((1, d), lambda i: (0, 0)),
            pl.BlockSpec((d, 2 * XA_W), lambda i: (0, 0)),
        ],
        out_specs=pl.BlockSpec((None, m, 2 * XA_W), lambda i: (i, 0, 0)),
        out_shape=jax.ShapeDtypeStruct((b, m, 2 * XA_W), BF16),
        compiler_params=_params(("arbitrary",)),
        name="memkv",
    )(mem, g_mem.reshape(1, d), w_bf)


def _inproj_kernel(x_ref, g_ref, w_ref, rgx_ref, rgg_ref, rest_ref):
    hn = _rms(x_ref[...], g_ref[...]).astype(BF16)
    rgx_ref[...] = jnp.dot(hn, w_ref[:, 0:RG_W], preferred_element_type=F32)
    rgg_ref[...] = jnp.dot(hn, w_ref[:, RG_W:2 * RG_W], preferred_element_type=F32)
    q_scale = {0: (DA_HEAD_DIM ** -0.5) * LOG2E, 4: (XA_HEAD_DIM ** -0.5) * LOG2E}
    for n in range(REST_W // 512):
        lo = 2 * RG_W + n * 512
        acc = jnp.dot(hn, w_ref[:, lo:lo + 512], preferred_element_type=F32)
        if n in q_scale:
            acc = acc * q_scale[n]
        rest_ref[:, n * 512:(n + 1) * 512] = acc.astype(BF16)


def _inproj(x, g_pre, w_bf, tm):
    b, s, d = x.shape
    d_in = w_bf.shape[1]
    return pl.pallas_call(
        _inproj_kernel,
        grid=(b, s // tm),
        in_specs=[
            pl.BlockSpec((None, tm, d), lambda i, j: (i, j, 0)),
            pl.BlockSpec((1, d), lambda i, j: (0, 0)),
            pl.BlockSpec((d, d_in), lambda i, j: (0, 0)),
        ],
        out_specs=[
            pl.BlockSpec((tm, RG_W), lambda i, j: (j, i)),
            pl.BlockSpec((tm, RG_W), lambda i, j: (j, i)),
            pl.BlockSpec((None, tm, REST_W), lambda i, j: (i, j, 0)),
        ],
        out_shape=[
            jax.ShapeDtypeStruct((s, b * RG_W), F32),
            jax.ShapeDtypeStruct((s, b * RG_W), F32),
            jax.ShapeDtypeStruct((b, s, REST_W), BF16),
        ],
        compiler_params=_params(("arbitrary", "arbitrary")),
        name="inproj",
    )(x, g_pre.reshape(1, d), w_bf)


def _rglru_kernel(x_ref, g_ref, cw_ref, cb_ref, wa_ref, ba_ref, wx_ref, bx_ref, lam_ref,
                  y_ref, xp_ref, a_ref, u_ref, h_ref, *, ts, nb):
    step = pl.program_id(0)
    halo = CONV_WIDTH - 1

    @pl.when(step == 0)
    def _():
        xp_ref[0:halo] = jnp.zeros((halo, nb, RG_W), F32)
        h_ref[...] = jnp.zeros((nb, RG_W), F32)

    @pl.when(step > 0)
    def _():
        xp_ref[0:halo] = xp_ref[ts:ts + halo]

    xp_ref[halo:halo + ts] = x_ref[...]

    xc = cb_ref[...].reshape(1, 1, RG_W)
    for k in range(CONV_WIDTH):
        xc = xc + cw_ref[k:k + 1, :].reshape(1, 1, RG_W) * xp_ref[k:k + ts]

    xb = xc.reshape(ts * nb, RG_W).astype(BF16)
    r = jax.nn.sigmoid(jnp.dot(xb, wa_ref[...], preferred_element_type=F32) + ba_ref[...])
    i = jax.nn.sigmoid(jnp.dot(xb, wx_ref[...], preferred_element_type=F32) + bx_ref[...])
    r = r.reshape(ts, nb, RG_W)
    i = i.reshape(ts, nb, RG_W)

    z = -lam_ref[...]
    softplus = jnp.maximum(z, 0.0) + jnp.log1p(jnp.exp(-jnp.abs(z)))
    log_a = (-LRU_C * r) * softplus.reshape(1, 1, RG_W)
    a = jnp.exp(log_a)
    mult = jnp.sqrt(jnp.maximum(1.0 - a * a, 0.0))
    a_ref[...] = a
    u_ref[...] = mult * (i * xc)

    def body(t, h):
        h = a_ref[t] * h + u_ref[t]
        u_ref[t] = h
        return h

    h_ref[...] = lax.fori_loop(0, ts, body, h_ref[...], unroll=8)
    y_ref[...] = u_ref[...] * _silu(g_ref[...])


def _block_diag(w):
    h, di, dj = w.shape
    eye = jnp.eye(h, dtype=w.dtype)
    return (eye[:, None, :, None] * w[:, :, None, :]).reshape(h * di, h * dj)


def _rglru(rgx, rgg, conv_w, conv_b, w_rg_a, b_rg_a, w_rg_x, b_rg_x, lru_lambda, nb, ts):
    s = rgx.shape[0]
    x3 = rgx.reshape(s, nb, RG_W)
    g3 = rgg.reshape(s, nb, RG_W)
    wa = _block_diag(w_rg_a).astype(BF16)
    wx = _block_diag(w_rg_x).astype(BF16)
    row = lambda v: v.reshape(1, RG_W)
    const2 = lambda shape: pl.BlockSpec(shape, lambda i: (0, 0))
    tile = pl.BlockSpec((ts, nb, RG_W), lambda i: (i, 0, 0))
    y3 = pl.pallas_call(
        functools.partial(_rglru_kernel, ts=ts, nb=nb),
        grid=(s // ts,),
        in_specs=[tile, tile, const2((CONV_WIDTH, RG_W)), const2((1, RG_W)),
                  const2((RG_W, RG_W)), const2((1, RG_W)), const2((RG_W, RG_W)), const2((1, RG_W)),
                  const2((1, RG_W))],
        out_specs=tile,
        out_shape=jax.ShapeDtypeStruct((s, nb, RG_W), F32),
        scratch_shapes=[
            pltpu.VMEM((ts + CONV_WIDTH - 1, nb, RG_W), F32),
            pltpu.VMEM((ts, nb, RG_W), F32),
            pltpu.VMEM((ts, nb, RG_W), F32),
            pltpu.VMEM((nb, RG_W), F32),
        ],
        compiler_params=_params(("arbitrary",)),
        name="rglru",
    )(x3, g3, conv_w, row(conv_b), wa, row(b_rg_a), wx, row(b_rg_x), row(lru_lambda))
    return y3.reshape(s, nb * RG_W)


def _diffattn_kernel(q_ref, k_ref, v_ref, g_ref, lq1_ref, lk1_ref, lq2_ref, lk2_ref, gs_ref,
                     o_ref, vt_ref, *, tq):
    hd = DA_V_DIM
    s_len = q_ref.shape[0]
    nq = s_len // tq
    w = 2 * tq

    vt_ref[0:hd, :] = v_ref[...].astype(F32).T.astype(BF16)
    vt_ref[hd:hd + BF16_ROWS, :] = jnp.ones((BF16_ROWS, s_len), BF16)

    lam = (jnp.exp(jnp.sum(lq1_ref[...] * lk1_ref[...], axis=-1, keepdims=True))
           - jnp.exp(jnp.sum(lq2_ref[...] * lk2_ref[...], axis=-1, keepdims=True))
           + LAMBDA_INIT)
    lane = lax.broadcasted_iota(jnp.int32, (tq, hd), 1)
    key = lax.broadcasted_iota(jnp.int32, (tq, w), 0)
    qry = lax.broadcasted_iota(jnp.int32, (tq, w), 1) & (tq - 1)
    causal = key <= qry

    for qb in range(nq):
        rows = slice(qb * tq, (qb + 1) * tq)
        kv = (qb + 1) * tq
        q = q_ref[rows, :].astype(F32)
        qq = jnp.concatenate([jnp.where(lane < DA_HEAD_DIM, q, 0.0),
                              jnp.where(lane >= DA_HEAD_DIM, q, 0.0)], axis=0)
        qqt = qq.T.astype(BF16)
        s_diag = jnp.dot(k_ref[kv - tq:kv, :], qqt, preferred_element_type=F32)
        parts = [jnp.where(causal, s_diag, -jnp.inf)]
        if qb > 0:
            parts.insert(0, jnp.dot(k_ref[0:kv - tq, :], qqt, preferred_element_type=F32))
        s3 = jnp.concatenate(parts, axis=0).reshape(kv // SUBLANES, SUBLANES, w)
        m = jnp.max(jnp.max(s3, axis=0), axis=0, keepdims=True)
        p = jnp.exp2(s3 - m[None]).reshape(kv, w).astype(BF16)
        pv = jnp.dot(vt_ref[:, 0:kv], p, preferred_element_type=F32)
        ot = pv[0:hd] / pv[hd:hd + 1]
        od = (ot[:, 0:tq] - lam * ot[:, tq:w]).T
        o = _rms(od, gs_ref[...]) * (1.0 - LAMBDA_INIT)
        o_ref[rows, :] = (o * _silu(g_ref[rows, :].astype(F32))).astype(BF16)


def _diffattn(rest, lq1, lk1, lq2, lk2, g_subln, tq):
    b, s, _ = rest.shape
    hd = DA_V_DIM
    vec = lambda v: v.reshape(1, -1)
    cvec = lambda n: pl.BlockSpec((1, n), lambda bi, h: (0, 0))
    col = lambda c: pl.BlockSpec((None, s, hd), lambda bi, h: (bi, 0, c * DA_HEADS + h))
    return pl.pallas_call(
        functools.partial(_diffattn_kernel, tq=tq),
        grid=(b, DA_HEADS),
        in_specs=[col(0), col(1), col(2), col(3),
                  cvec(DA_HEAD_DIM), cvec(DA_HEAD_DIM), cvec(DA_HEAD_DIM), cvec(DA_HEAD_DIM),
                  cvec(DA_V_DIM)],
        out_specs=pl.BlockSpec((None, s, hd), lambda bi, h: (bi, 0, h)),
        out_shape=jax.ShapeDtypeStruct((b, s, DA_W), BF16),
        scratch_shapes=[pltpu.VMEM((hd + BF16_ROWS, s), BF16)],
        compiler_params=_params(("arbitrary", "arbitrary")),
        name="diffattn",
    )(rest, rest, rest, rest, vec(lq1), vec(lk1), vec(lq2), vec(lk2), vec(g_subln))


def _memxattn_kernel(q_ref, g_ref, km_ref, vm_ref, o_ref):
    for h in range(XA_HEADS):
        sl = slice(h * XA_HEAD_DIM, (h + 1) * XA_HEAD_DIM)
        s = _nt_dot(q_ref[:, sl], km_ref[:, sl])
        m = jnp.max(s, axis=-1, keepdims=True)
        p = jnp.exp2(s - m)
        l = jnp.sum(p, axis=-1, keepdims=True)
        o = jnp.dot(p.astype(BF16), vm_ref[:, sl], preferred_element_type=F32) / l
        o_ref[:, sl] = (o * _silu(g_ref[:, sl].astype(F32))).astype(BF16)


def _memxattn(rest, kv, tq):
    b, s, _ = rest.shape
    m = kv.shape[1]
    return pl.pallas_call(
        _memxattn_kernel,
        grid=(b, s // tq),
        in_specs=[
            pl.BlockSpec((None, tq, XA_W), lambda bi, i: (bi, i, 4)),
            pl.BlockSpec((None, tq, XA_W), lambda bi, i: (bi, i, 5)),
            pl.BlockSpec((None, m, XA_W), lambda bi, i: (bi, 0, 0)),
            pl.BlockSpec((None, m, XA_W), lambda bi, i: (bi, 0, 1)),
        ],
        out_specs=pl.BlockSpec((None, tq, XA_W), lambda bi, i: (bi, i, 0)),
        out_shape=jax.ShapeDtypeStruct((b, s, XA_W), BF16),
        compiler_params=_params(("arbitrary", "arbitrary")),
        name="memxattn",
    )(rest, rest, kv, kv)


def _outproj_kernel(yrg_ref, yda_ref, yxa_ref, w_ref, x_ref, g_ref, o_ref):
    y = jnp.dot(yrg_ref[...].astype(BF16), w_ref[0:RG_W, :], preferred_element_type=F32)
    y = y + jnp.dot(yda_ref[...], w_ref[RG_W:RG_W + DA_W, :], preferred_element_type=F32)
    y = y + jnp.dot(yxa_ref[...], w_ref[RG_W + DA_W:, :], preferred_element_type=F32)
    o_ref[...] = x_ref[...] + _rms(y, g_ref[...])


def _outproj(y_rg, y_da, y_xa, w_bf, x, g_post, tm):
    b, s, d = x.shape
    d_mix = w_bf.shape[0]
    return pl.pallas_call(
        _outproj_kernel,
        grid=(b, s // tm),
        in_specs=[
            pl.BlockSpec((tm, RG_W), lambda i, j: (j, i)),
            pl.BlockSpec((None, tm, DA_W), lambda i, j: (i, j, 0)),
            pl.BlockSpec((None, tm, XA_W), lambda i, j: (i, j, 0)),
            pl.BlockSpec((d_mix, d), lambda i, j: (0, 0)),
            pl.BlockSpec((None, tm, d), lambda i, j: (i, j, 0)),
            pl.BlockSpec((1, d), lambda i, j: (0, 0)),
        ],
        out_specs=pl.BlockSpec((None, tm, d), lambda i, j: (i, j, 0)),
        out_shape=jax.ShapeDtypeStruct((b, s, d), F32),
        compiler_params=_params(("arbitrary", "arbitrary")),
        name="outproj",
    )(y_rg, y_da, y_xa, w_bf, x, g_post.reshape(1, d))


def kernel(x, mem, g_pre, g_mem, w_in, w_mem_kv, conv_w, conv_b, w_rg_a, b_rg_a, w_rg_x, b_rg_x,
           lru_lambda, lambda_q1, lambda_k1, lambda_q2, lambda_k2, g_subln, w_out, g_post):
    b = x.shape[0]
    kv = _memkv(mem, g_mem, w_mem_kv.astype(BF16))
    rgx, rgg, rest = _inproj(x, g_pre, w_in.astype(BF16), tm=512)
    y_rg = _rglru(rgx, rgg, conv_w, conv_b, w_rg_a, b_rg_a, w_rg_x, b_rg_x, lru_lambda, nb=b, ts=128)
    y_da = _diffattn(rest, lambda_q1, lambda_k1, lambda_q2, lambda_k2, g_subln, tq=256)
    y_xa = _memxattn(rest, kv, tq=512)
    return _outproj(y_rg, y_da, y_xa, w_out.astype(BF16), x, g_post, tm=512)
```

```python
import functools
import math

import jax
import jax.numpy as jnp
from jax import lax
from jax.experimental import pallas as pl
from jax.experimental.pallas import tpu as pltpu

F32 = jnp.float32
BF16 = jnp.bfloat16

RMS_EPS = 1e-6
N_MEM = 256
RG_HEADS = 8
RG_HEAD_DIM = 64
RG_W = RG_HEADS * RG_HEAD_DIM
CONV_WIDTH = 4
LRU_C = 8.0
DA_HEADS = 4
DA_HEAD_DIM = 64
DA_V_DIM = 2 * DA_HEAD_DIM
DA_W = DA_HEADS * DA_V_DIM
LAMBDA_INIT = 0.8 - 0.6 * math.exp(-0.3 * (1 - 1))
XA_HEADS = 4
XA_HEAD_DIM = 128
XA_W = XA_HEADS * XA_HEAD_DIM
REST_W = 6 * 512
LOG2E = 1.4426950408889634

SUBLANES = 8
LANES = 128
BF16_ROWS = 16
VMEM_LIMIT_BYTES = 56 * 1024 * 1024


def _params(sem):
    return pltpu.CompilerParams(dimension_semantics=sem, vmem_limit_bytes=VMEM_LIMIT_BYTES)


def _rms(x, g):
    ms = jnp.mean(x * x, axis=-1, keepdims=True)
    return x * lax.rsqrt(ms + RMS_EPS) * g


def _silu(x):
    h = 0.5 * x
    return h * (1.0 + jnp.tanh(h))


def _nt_dot(a, b):
    return lax.dot_general(a, b, (((1,), (1,)), ((), ())), preferred_element_type=F32)


def _memkv_kernel(mem_ref, g_ref, w_ref, out_ref):
    mn = _rms(mem_ref[...], g_ref[...]).astype(BF16)
    out_ref[...] = jnp.dot(mn, w_ref[...], preferred_element_type=F32).astype(BF16)


def _memkv(mem, g_mem, w_bf):
    b, m, d = mem.shape
    return pl.pallas_call(
        _memkv_kernel,
        grid=(b,),
        in_specs=[
            pl.BlockSpec((None, m, d), lambda i: (i, 0, 0)),
            pl.BlockSpec((1, d), lambda i: (0, 0)),
            pl.BlockSpec((d, 2 * XA_W), lambda i: (0, 0)),
        ],
        out_specs=pl.BlockSpec((None, m, 2 * XA_W), lambda i: (i, 0, 0)),
        out_shape=jax.ShapeDtypeStruct((b, m, 2 * XA_W), BF16),
        compiler_params=_params(("arbitrary",)),
        name="memkv",
    )(mem, g_mem.reshape(1, d), w_bf)


def _tm_store(ref, b, nb, val):
    rows = val.shape[0]
    for c in range(RG_W // LANES):
        ref[c, pl.ds(b, rows, stride=nb), :] = val[:, c * LANES:(c + 1) * LANES]


def _tm_load(ref, b, nb, rows):
    return jnp.concatenate(
        [ref[c, pl.ds(b, rows, stride=nb), :] for c in range(RG_W // LANES)], axis=1)


def _inproj_kernel(x_ref, g_ref, w_ref, rgx_ref, rgg_ref, rest_ref, *, nb):
    b = pl.program_id(1)
    hn = _rms(x_ref[...], g_ref[...]).astype(BF16)
    _tm_store(rgx_ref, b, nb, jnp.dot(hn, w_ref[:, 0:RG_W], preferred_element_type=F32))
    _tm_store(rgg_ref, b, nb, jnp.dot(hn, w_ref[:, RG_W:2 * RG_W], preferred_element_type=F32))
    q_scale = {0: (DA_HEAD_DIM ** -0.5) * LOG2E, 4: (XA_HEAD_DIM ** -0.5) * LOG2E}
    for n in range(REST_W // 512):
        lo = 2 * RG_W + n * 512
        acc = jnp.dot(hn, w_ref[:, lo:lo + 512], preferred_element_type=F32)
        if n in q_scale:
            acc = acc * q_scale[n]
        rest_ref[:, n * 512:(n + 1) * 512] = acc.astype(BF16)


def _inproj(x, g_pre, w_bf, tm):
    b, s, d = x.shape
    d_in = w_bf.shape[1]
    tm_block = pl.BlockSpec((RG_W // LANES, tm * b, LANES), lambda j, i: (0, j, 0))
    tm_shape = jax.ShapeDtypeStruct((RG_W // LANES, s * b, LANES), F32)
    return pl.pallas_call(
        functools.partial(_inproj_kernel, nb=b),
        grid=(s // tm, b),
        in_specs=[
            pl.BlockSpec((None, tm, d), lambda j, i: (i, j, 0)),
            pl.BlockSpec((1, d), lambda j, i: (0, 0)),
            pl.BlockSpec((d, d_in), lambda j, i: (0, 0)),
        ],
        out_specs=[tm_block, tm_block, pl.BlockSpec((None, tm, REST_W), lambda j, i: (i, j, 0))],
        out_shape=[tm_shape, tm_shape, jax.ShapeDtypeStruct((b, s, REST_W), BF16)],
        compiler_params=_params(("arbitrary", "arbitrary")),
        name="inproj",
    )(x, g_pre.reshape(1, d), w_bf)


def _rglru_kernel(x_ref, g_ref, cw_ref, cb_ref, wa_ref, ba_ref, wx_ref, bx_ref, lam_ref,
                  y_ref, xp_ref, a_ref, u_ref, h_ref, *, ts, nb):
    step = pl.program_id(0)
    halo = CONV_WIDTH - 1
    slabs = [slice(c * LANES, (c + 1) * LANES) for c in range(RG_W // LANES)]

    @pl.when(step == 0)
    def _():
        xp_ref[0:halo] = jnp.zeros((halo, nb, RG_W), F32)
        h_ref[...] = jnp.zeros((nb, RG_W), F32)

    @pl.when(step > 0)
    def _():
        xp_ref[0:halo] = xp_ref[ts:ts + halo]

    for c, sl in enumerate(slabs):
        xp_ref[halo:halo + ts, :, sl] = x_ref[c].reshape(ts, nb, LANES)

    xh = 0.5 * cb_ref[...].reshape(1, 1, RG_W)
    for k in range(CONV_WIDTH):
        xh = xh + (0.5 * cw_ref[k:k + 1, :]).reshape(1, 1, RG_W) * xp_ref[k:k + ts]

    xb = xh.reshape(ts * nb, RG_W).astype(BF16)
    tr = jnp.tanh(jnp.dot(xb, wa_ref[...], preferred_element_type=F32) + 0.5 * ba_ref[...])
    ti = jnp.tanh(jnp.dot(xb, wx_ref[...], preferred_element_type=F32) + 0.5 * bx_ref[...])
    tr = tr.reshape(ts, nb, RG_W)
    ti = ti.reshape(ts, nb, RG_W)

    z = -lam_ref[...]
    softplus = jnp.maximum(z, 0.0) + jnp.log1p(jnp.exp(-jnp.abs(z)))
    half_rate = ((-0.5 * LRU_C * LOG2E) * softplus).reshape(1, 1, RG_W)
    a = jnp.exp2(tr * half_rate + half_rate)
    d = jnp.maximum(1.0 - a * a, 0.0)
    mult = jnp.where(d > 0.0, d * lax.rsqrt(d), 0.0)
    a_ref[...] = a
    u_ref[...] = mult * (xh * (1.0 + ti))

    def body(t, h):
        h = a_ref[t] * h + u_ref[t]
        u_ref[t] = h
        return h

    h_ref[...] = lax.fori_loop(0, ts, body, h_ref[...], unroll=8)
    for c, sl in enumerate(slabs):
        gh = 0.5 * g_ref[c].reshape(ts, nb, LANES)
        silu = gh * (1.0 + jnp.tanh(gh))
        y_ref[c] = (u_ref[:, :, sl] * silu).reshape(ts * nb, LANES)


def _block_diag(w):
    h, di, dj = w.shape
    eye = jnp.eye(h, dtype=w.dtype)
    return (eye[:, None, :, None] * w[:, :, None, :]).reshape(h * di, h * dj)


def _rglru(rgx, rgg, conv_w, conv_b, w_rg_a, b_rg_a, w_rg_x, b_rg_x, lru_lambda, nb, ts):
    s = rgx.shape[1] // nb
    wa = _block_diag(w_rg_a).astype(BF16)
    wx = _block_diag(w_rg_x).astype(BF16)
    row = lambda v: v.reshape(1, RG_W)
    const2 = lambda shape: pl.BlockSpec(shape, lambda i: (0, 0))
    tile = pl.BlockSpec((RG_W // LANES, ts * nb, LANES), lambda i: (0, i, 0))
    return pl.pallas_call(
        functools.partial(_rglru_kernel, ts=ts, nb=nb),
        grid=(s // ts,),
        in_specs=[tile, tile, const2((CONV_WIDTH, RG_W)), const2((1, RG_W)),
                  const2((RG_W, RG_W)), const2((1, RG_W)), const2((RG_W, RG_W)), const2((1, RG_W)),
                  const2((1, RG_W))],
        out_specs=tile,
        out_shape=jax.ShapeDtypeStruct(rgx.shape, F32),
        scratch_shapes=[
            pltpu.VMEM((ts + CONV_WIDTH - 1, nb, RG_W), F32),
            pltpu.VMEM((ts, nb, RG_W), F32),
            pltpu.VMEM((ts, nb, RG_W), F32),
            pltpu.VMEM((nb, RG_W), F32),
        ],
        compiler_params=_params(("arbitrary",)),
        name="rglru",
    )(rgx, rgg, conv_w, row(conv_b), wa, row(b_rg_a), wx, row(b_rg_x), row(lru_lambda))


def _diffattn_kernel(q_ref, k_ref, v_ref, g_ref, lq1_ref, lk1_ref, lq2_ref, lk2_ref, gs_ref,
                     o_ref, vt_ref, *, tq):
    hd = DA_V_DIM
    s_len = q_ref.shape[0]
    nq = s_len // tq
    w = 2 * tq

    vt_ref[0:hd, :] = v_ref[...].astype(F32).T.astype(BF16)
    vt_ref[hd:hd + BF16_ROWS, :] = jnp.ones((BF16_ROWS, s_len), BF16)

    lam = (jnp.exp(jnp.sum(lq1_ref[...] * lk1_ref[...], axis=-1, keepdims=True))
           - jnp.exp(jnp.sum(lq2_ref[...] * lk2_ref[...], axis=-1, keepdims=True))
           + LAMBDA_INIT)
    lane = lax.broadcasted_iota(jnp.int32, (tq, hd), 1)
    key = lax.broadcasted_iota(jnp.int32, (tq, w), 0)
    qry = lax.broadcasted_iota(jnp.int32, (tq, w), 1) & (tq - 1)
    causal = key <= qry

    for qb in range(nq):
        rows = slice(qb * tq, (qb + 1) * tq)
        kv = (qb + 1) * tq
        q = q_ref[rows, :].astype(F32)
        qq = jnp.concatenate([jnp.where(lane < DA_HEAD_DIM, q, 0.0),
                              jnp.where(lane >= DA_HEAD_DIM, q, 0.0)], axis=0)
        qqt = qq.T.astype(BF16)
        s_diag = jnp.dot(k_ref[kv - tq:kv, :], qqt, preferred_element_type=F32)
        parts = [jnp.where(causal, s_diag, -jnp.inf)]
        if qb > 0:
            parts.insert(0, jnp.dot(k_ref[0:kv - tq, :], qqt, preferred_element_type=F32))
        s3 = jnp.concatenate(parts, axis=0).reshape(kv // SUBLANES, SUBLANES, w)
        m = jnp.max(jnp.max(s3, axis=0), axis=0, keepdims=True)
        p = jnp.exp2(s3 - m[None]).reshape(kv, w).astype(BF16)
        pv = jnp.dot(vt_ref[:, 0:kv], p, preferred_element_type=F32)
        ot = pv[0:hd] / pv[hd:hd + 1]
        od = (ot[:, 0:tq] - lam * ot[:, tq:w]).T
        o = _rms(od, gs_ref[...]) * (1.0 - LAMBDA_INIT)
        o_ref[rows, :] = (o * _silu(g_ref[rows, :].astype(F32))).astype(BF16)


def _diffattn(rest, lq1, lk1, lq2, lk2, g_subln, tq):
    b, s, _ = rest.shape
    hd = DA_V_DIM
    vec = lambda v: v.reshape(1, -1)
    cvec = lambda n: pl.BlockSpec((1, n), lambda bi, h: (0, 0))
    col = lambda c: pl.BlockSpec((None, s, hd), lambda bi, h: (bi, 0, c * DA_HEADS + h))
    return pl.pallas_call(
        functools.partial(_diffattn_kernel, tq=tq),
        grid=(b, DA_HEADS),
        in_specs=[col(0), col(1), col(2), col(3),
                  cvec(DA_HEAD_DIM), cvec(DA_HEAD_DIM), cvec(DA_HEAD_DIM), cvec(DA_HEAD_DIM),
                  cvec(DA_V_DIM)],
        out_specs=pl.BlockSpec((None, s, hd), lambda bi, h: (bi, 0, h)),
        out_shape=jax.ShapeDtypeStruct((b, s, DA_W), BF16),
        scratch_shapes=[pltpu.VMEM((hd + BF16_ROWS, s), BF16)],
        compiler_params=_params(("arbitrary", "arbitrary")),
        name="diffattn",
    )(rest, rest, rest, rest, vec(lq1), vec(lk1), vec(lq2), vec(lk2), vec(g_subln))


def _memxattn_kernel(q_ref, g_ref, km_ref, vm_ref, o_ref):
    for h in range(XA_HEADS):
        sl = slice(h * XA_HEAD_DIM, (h + 1) * XA_HEAD_DIM)
        s = _nt_dot(q_ref[:, sl], km_ref[:, sl])
        m = jnp.max(s, axis=-1, keepdims=True)
        p = jnp.exp2(s - m)
        l = jnp.sum(p, axis=-1, keepdims=True)
        o = jnp.dot(p.astype(BF16), vm_ref[:, sl], preferred_element_type=F32) / l
        o_ref[:, sl] = (o * _silu(g_ref[:, sl].astype(F32))).astype(BF16)


def _memxattn(rest, kv, tq):
    b, s, _ = rest.shape
    m = kv.shape[1]
    return pl.pallas_call(
        _memxattn_kernel,
        grid=(b, s // tq),
        in_specs=[
            pl.BlockSpec((None, tq, XA_W), lambda bi, i: (bi, i, 4)),
            pl.BlockSpec((None, tq, XA_W), lambda bi, i: (bi, i, 5)),
            pl.BlockSpec((None, m, XA_W), lambda bi, i: (bi, 0, 0)),
            pl.BlockSpec((None, m, XA_W), lambda bi, i: (bi, 0, 1)),
        ],
        out_specs=pl.BlockSpec((None, tq, XA_W), lambda bi, i: (bi, i, 0)),
        out_shape=jax.ShapeDtypeStruct((b, s, XA_W), BF16),
        compiler_params=_params(("arbitrary", "arbitrary")),
        name="memxattn",
    )(rest, rest, kv, kv)


def _outproj_kernel(yrg_ref, yda_ref, yxa_ref, w_ref, x_ref, g_ref, o_ref, *, nb):
    tm = x_ref.shape[0]
    yrg = _tm_load(yrg_ref, pl.program_id(1), nb, tm).astype(BF16)
    y = jnp.dot(yrg, w_ref[0:RG_W, :], preferred_element_type=F32)
    y = y + jnp.dot(yda_ref[...], w_ref[RG_W:RG_W + DA_W, :], preferred_element_type=F32)
    y = y + jnp.dot(yxa_ref[...], w_ref[RG_W + DA_W:, :], preferred_element_type=F32)
    o_ref[...] = x_ref[...] + _rms(y, g_ref[...])


def _outproj(y_rg, y_da, y_xa, w_bf, x, g_post, tm):
    b, s, d = x.shape
    d_mix = w_bf.shape[0]
    return pl.pallas_call(
        functools.partial(_outproj_kernel, nb=b),
        grid=(s // tm, b),
        in_specs=[
            pl.BlockSpec((RG_W // LANES, tm * b, LANES), lambda j, i: (0, j, 0)),
            pl.BlockSpec((None, tm, DA_W), lambda j, i: (i, j, 0)),
            pl.BlockSpec((None, tm, XA_W), lambda j, i: (i, j, 0)),
            pl.BlockSpec((d_mix, d), lambda j, i: (0, 0)),
            pl.BlockSpec((None, tm, d), lambda j, i: (i, j, 0)),
            pl.BlockSpec((1, d), lambda j, i: (0, 0)),
        ],
        out_specs=pl.BlockSpec((None, tm, d), lambda j, i: (i, j, 0)),
        out_shape=jax.ShapeDtypeStruct((b, s, d), F32),
        compiler_params=_params(("arbitrary", "arbitrary")),
        name="outproj",
    )(y_rg, y_da, y_xa, w_bf, x, g_post.reshape(1, d))


def kernel(x, mem, g_pre, g_mem, w_in, w_mem_kv, conv_w, conv_b, w_rg_a, b_rg_a, w_rg_x, b_rg_x,
           lru_lambda, lambda_q1, lambda_k1, lambda_q2, lambda_k2, g_subln, w_out, g_post):
    b = x.shape[0]
    kv = _memkv(mem, g_mem, w_mem_kv.astype(BF16))
    rgx, rgg, rest = _inproj(x, g_pre, w_in.astype(BF16), tm=512)
    y_rg = _rglru(rgx, rgg, conv_w, conv_b, w_rg_a, b_rg_a, w_rg_x, b_rg_x, lru_lambda, nb=b, ts=128)
    y_da = _diffattn(rest, lambda_q1, lambda_k1, lambda_q2, lambda_k2, g_subln, tq=256)
    y_xa = _memxattn(rest, kv, tq=512)
    return _outproj(y_rg, y_da, y_xa, w_out.astype(BF16), x, g_post, tm=512)
```

```python
import functools
import math

import jax
import jax.numpy as jnp
from jax import lax
from jax.experimental import pallas as pl
from jax.experimental.pallas import tpu as pltpu

F32 = jnp.float32
BF16 = jnp.bfloat16

RMS_EPS = 1e-6
N_MEM = 256
RG_HEADS = 8
RG_HEAD_DIM = 64
RG_W = RG_HEADS * RG_HEAD_DIM
CONV_WIDTH = 4
LRU_C = 8.0
DA_HEADS = 4
DA_HEAD_DIM = 64
DA_V_DIM = 2 * DA_HEAD_DIM
DA_W = DA_HEADS * DA_V_DIM
LAMBDA_INIT = 0.8 - 0.6 * math.exp(-0.3 * (1 - 1))
XA_HEADS = 4
XA_HEAD_DIM = 128
XA_W = XA_HEADS * XA_HEAD_DIM
LOG2E = 1.4426950408889634

PROJ_CHUNK = 512
(C_RG_X, C_RG_G, C_DA_Q, C_DA_K, C_DA_V, C_DA_G, C_XA_Q, C_XA_G) = range(8)
D_IN = 8 * PROJ_CHUNK

SUBLANES = 8
LANES = 128
BF16_ROWS = 16
MXU_DIM = 256
VMEM_LIMIT_BYTES = 56 * 1024 * 1024


def _params(sem):
    return pltpu.CompilerParams(dimension_semantics=sem, vmem_limit_bytes=VMEM_LIMIT_BYTES)


def _rms(x, g):
    ms = jnp.mean(x * x, axis=-1, keepdims=True)
    return x * lax.rsqrt(ms + RMS_EPS) * g


def _silu(x):
    h = 0.5 * x
    return h * (1.0 + jnp.tanh(h))


def _nt_dot(a, b):
    return lax.dot_general(a, b, (((1,), (1,)), ((), ())), preferred_element_type=F32)


def _memkv_kernel(mem_ref, g_ref, w_ref, out_ref):
    mn = _rms(mem_ref[...], g_ref[...]).astype(BF16)
    out_ref[...] = jnp.dot(mn, w_ref[...], preferred_element_type=F32).astype(BF16)


def _memkv(mem, g_mem, w_bf):
    b, m, d = mem.shape
    return pl.pallas_call(
        _memkv_kernel,
        grid=(b,),
        in_specs=[
            pl.BlockSpec((None, m, d), lambda i: (i, 0, 0)),
            pl.BlockSpec((1, d), lambda i: (0, 0)),
            pl.BlockSpec((d, 2 * XA_W), lambda i: (0, 0)),
        ],
        out_specs=pl.BlockSpec((None, m, 2 * XA_W), lambda i: (i, 0, 0)),
        out_shape=jax.ShapeDtypeStruct((b, m, 2 * XA_W), BF16),
        compiler_params=_params(("arbitrary",)),
        name="memkv",
    )(mem, g_mem.reshape(1, d), w_bf)


def _inproj_kernel(x_ref, g_ref, w_ref, proj_ref):
    hn = _rms(x_ref[...], g_ref[...]).astype(BF16)
    q_scale = {C_DA_Q: (DA_HEAD_DIM ** -0.5) * LOG2E, C_XA_Q: (XA_HEAD_DIM ** -0.5) * LOG2E}
    for n in range(D_IN // PROJ_CHUNK):
        cols = slice(n * PROJ_CHUNK, (n + 1) * PROJ_CHUNK)
        acc = jnp.dot(hn, w_ref[:, cols], preferred_element_type=F32)
        if n in q_scale:
            acc = acc * q_scale[n]
        proj_ref[:, cols] = acc.astype(BF16)


def _inproj(x, g_pre, w_bf, tm):
    b, s, d = x.shape
    return pl.pallas_call(
        _inproj_kernel,
        grid=(b, s // tm),
        in_specs=[
            pl.BlockSpec((None, tm, d), lambda i, j: (i, j, 0)),
            pl.BlockSpec((1, d), lambda i, j: (0, 0)),
            pl.BlockSpec((d, D_IN), lambda i, j: (0, 0)),
        ],
        out_specs=pl.BlockSpec((None, tm, D_IN), lambda i, j: (i, j, 0)),
        out_shape=jax.ShapeDtypeStruct((b, s, D_IN), BF16),
        compiler_params=_params(("arbitrary", "arbitrary")),
        name="inproj",
    )(x, g_pre.reshape(1, d), w_bf)


RG_SUB = MXU_DIM // 8


def _rglru_kernel(p_ref, cw_ref, cb_ref, wa_ref, ba_ref, wx_ref, bx_ref, lam_ref,
                  y_ref, xp_ref, g_scr, a_ref, u_ref, h_ref, *, ts, nb):
    step = pl.program_id(0)
    halo = CONV_WIDTH - 1
    sub = RG_SUB
    rows = sub * nb
    assert rows == MXU_DIM and nb == SUBLANES and ts % sub == 0

    @pl.when(step == 0)
    def _():
        xp_ref[0:halo] = jnp.zeros((halo, nb, RG_W), F32)
        h_ref[...] = jnp.zeros((nb, RG_W), F32)

    @pl.when(step > 0)
    def _():
        xp_ref[0:halo] = xp_ref[ts:ts + halo]

    r = lax.broadcasted_iota(jnp.int32, (rows, rows), 0)
    c = lax.broadcasted_iota(jnp.int32, (rows, rows), 1)
    log_nb, log_sub = nb.bit_length() - 1, sub.bit_length() - 1
    to_time_major = (c == (r & (nb - 1)) * sub + (r >> log_nb)).astype(BF16)
    to_batch_major = (c == (r & (sub - 1)) * nb + (r >> log_sub)).astype(BF16)

    for sb in range(ts // sub):
        t0 = sb * sub
        blk = p_ref[:, t0:t0 + sub, :].reshape(rows, 2 * RG_W)
        tm = jnp.dot(to_time_major, blk, preferred_element_type=F32).reshape(sub, nb, 2 * RG_W)
        xp_ref[halo + t0:halo + t0 + sub] = tm[:, :, 0:RG_W]
        g_scr[t0:t0 + sub] = tm[:, :, RG_W:2 * RG_W]

    xh = 0.5 * cb_ref[...].reshape(1, 1, RG_W)
    for k in range(CONV_WIDTH):
        xh = xh + (0.5 * cw_ref[k:k + 1, :]).reshape(1, 1, RG_W) * xp_ref[k:k + ts]

    xb = xh.reshape(ts * nb, RG_W).astype(BF16)

    def gate(w_ref, b_ref):
        z = [jnp.dot(xb[:, m * MXU_DIM:(m + 1) * MXU_DIM], w_ref[m], preferred_element_type=F32)
             for m in range(RG_W // MXU_DIM)]
        return jnp.tanh(jnp.concatenate(z, axis=1) + 0.5 * b_ref[...]).reshape(ts, nb, RG_W)

    tr = gate(wa_ref, ba_ref)
    ti = gate(wx_ref, bx_ref)

    z = -lam_ref[...]
    softplus = jnp.maximum(z, 0.0) + jnp.log1p(jnp.exp(-jnp.abs(z)))
    half_rate = ((-0.5 * LRU_C * LOG2E) * softplus).reshape(1, 1, RG_W)
    a = jnp.exp2(tr * half_rate + half_rate)
    d = jnp.maximum(1.0 - a * a, 0.0)
    mult = jnp.where(d > 0.0, d * lax.rsqrt(d), 0.0)
    a_ref[...] = a
    u_ref[...] = mult * (xh * (1.0 + ti))

    def body(t, h):
        h = a_ref[t] * h + u_ref[t]
        u_ref[t] = h
        return h

    h_ref[...] = lax.fori_loop(0, ts, body, h_ref[...], unroll=8)

    for sb in range(ts // sub):
        t0 = sb * sub
        y = (u_ref[t0:t0 + sub] * _silu(g_scr[t0:t0 + sub])).reshape(rows, RG_W).astype(BF16)
        bm = jnp.dot(to_batch_major, y, preferred_element_type=F32)
        y_ref[:, t0:t0 + sub, :] = bm.reshape(nb, sub, RG_W).astype(BF16)


def _block_diag(w, group):
    h, di, dj = w.shape
    w = w.reshape(h // group, group, di, dj)
    eye = jnp.eye(group, dtype=w.dtype)
    return (eye[None, :, None, :, None] * w[:, :, :, None, :]).reshape(h // group, group * di, group * dj)


def _rglru(proj, conv_w, conv_b, w_rg_a, b_rg_a, w_rg_x, b_rg_x, lru_lambda, ts):
    nb, s, _ = proj.shape
    group = MXU_DIM // RG_HEAD_DIM
    n_blk = RG_HEADS // group
    wa = _block_diag(w_rg_a, group).astype(BF16)
    wx = _block_diag(w_rg_x, group).astype(BF16)
    row = lambda v: v.reshape(1, RG_W)
    const2 = lambda shape: pl.BlockSpec(shape, lambda i: (0, 0))
    const3 = lambda shape: pl.BlockSpec(shape, lambda i: (0, 0, 0))
    return pl.pallas_call(
        functools.partial(_rglru_kernel, ts=ts, nb=nb),
        grid=(s // ts,),
        in_specs=[pl.BlockSpec((nb, ts, 2 * RG_W), lambda i: (0, i, 0)),
                  const2((CONV_WIDTH, RG_W)), const2((1, RG_W)),
                  const3((n_blk, MXU_DIM, MXU_DIM)), const2((1, RG_W)),
                  const3((n_blk, MXU_DIM, MXU_DIM)), const2((1, RG_W)),
                  const2((1, RG_W))],
        out_specs=pl.BlockSpec((nb, ts, RG_W), lambda i: (0, i, 0)),
        out_shape=jax.ShapeDtypeStruct((nb, s, RG_W), BF16),
        scratch_shapes=[
            pltpu.VMEM((ts + CONV_WIDTH - 1, nb, RG_W), F32),
            pltpu.VMEM((ts, nb, RG_W), F32),
            pltpu.VMEM((ts, nb, RG_W), F32),
            pltpu.VMEM((ts, nb, RG_W), F32),
            pltpu.VMEM((nb, RG_W), F32),
        ],
        compiler_params=_params(("arbitrary",)),
        name="rglru",
    )(proj, conv_w, row(conv_b), wa, row(b_rg_a), wx, row(b_rg_x), row(lru_lambda))


def _diffattn_kernel(q_ref, k_ref, v_ref, g_ref, lq1_ref, lk1_ref, lq2_ref, lk2_ref, gs_ref,
                     o_ref, vt_ref, *, tq):
    hd = DA_V_DIM
    s_len = q_ref.shape[0]
    nq = s_len // tq
    w = 2 * tq

    vt_ref[0:hd, :] = v_ref[...].astype(F32).T.astype(BF16)
    vt_ref[hd:hd + BF16_ROWS, :] = jnp.ones((BF16_ROWS, s_len), BF16)

    lam = (jnp.exp(jnp.sum(lq1_ref[...] * lk1_ref[...], axis=-1, keepdims=True))
           - jnp.exp(jnp.sum(lq2_ref[...] * lk2_ref[...], axis=-1, keepdims=True))
           + LAMBDA_INIT)
    lane = lax.broadcasted_iota(jnp.int32, (tq, hd), 1)
    key = lax.broadcasted_iota(jnp.int32, (tq, w), 0)
    qry = lax.broadcasted_iota(jnp.int32, (tq, w), 1) & (tq - 1)
    causal = key <= qry

    for qb in range(nq):
        rows = slice(qb * tq, (qb + 1) * tq)
        kv = (qb + 1) * tq
        q = q_ref[rows, :].astype(F32)
        qq = jnp.concatenate([jnp.where(lane < DA_HEAD_DIM, q, 0.0),
                              jnp.where(lane >= DA_HEAD_DIM, q, 0.0)], axis=0)
        qqt = qq.T.astype(BF16)
        s_diag = jnp.dot(k_ref[kv - tq:kv, :], qqt, preferred_element_type=F32)
        parts = [jnp.where(causal, s_diag, -jnp.inf)]
        if qb > 0:
            parts.insert(0, jnp.dot(k_ref[0:kv - tq, :], qqt, preferred_element_type=F32))
        s3 = jnp.concatenate(parts, axis=0).reshape(kv // SUBLANES, SUBLANES, w)
        m = jnp.max(jnp.max(s3, axis=0), axis=0, keepdims=True)
        p = jnp.exp2(s3 - m[None]).reshape(kv, w).astype(BF16)
        pv = jnp.dot(vt_ref[:, 0:kv], p, preferred_element_type=F32)
        ot = pv[0:hd] / pv[hd:hd + 1]
        od = (ot[:, 0:tq] - lam * ot[:, tq:w]).T
        o = _rms(od, gs_ref[...]) * (1.0 - LAMBDA_INIT)
        o_ref[rows, :] = (o * _silu(g_ref[rows, :].astype(F32))).astype(BF16)


def _diffattn(proj, lq1, lk1, lq2, lk2, g_subln, tq):
    b, s, _ = proj.shape
    hd = DA_V_DIM
    per_chunk = PROJ_CHUNK // hd
    vec = lambda v: v.reshape(1, -1)
    cvec = lambda n: pl.BlockSpec((1, n), lambda bi, h: (0, 0))
    col = lambda chunk: pl.BlockSpec((None, s, hd), lambda bi, h: (bi, 0, chunk * per_chunk + h))
    return pl.pallas_call(
        functools.partial(_diffattn_kernel, tq=tq),
        grid=(b, DA_HEADS),
        in_specs=[col(C_DA_Q), col(C_DA_K), col(C_DA_V), col(C_DA_G),
                  cvec(DA_HEAD_DIM), cvec(DA_HEAD_DIM), cvec(DA_HEAD_DIM), cvec(DA_HEAD_DIM),
                  cvec(DA_V_DIM)],
        out_specs=pl.BlockSpec((None, s, hd), lambda bi, h: (bi, 0, h)),
        out_shape=jax.ShapeDtypeStruct((b, s, DA_W), BF16),
        scratch_shapes=[pltpu.VMEM((hd + BF16_ROWS, s), BF16)],
        compiler_params=_params(("arbitrary", "arbitrary")),
        name="diffattn",
    )(proj, proj, proj, proj, vec(lq1), vec(lk1), vec(lq2), vec(lk2), vec(g_subln))


def _memxattn_kernel(q_ref, g_ref, km_ref, vm_ref, o_ref):
    for h in range(XA_HEADS):
        sl = slice(h * XA_HEAD_DIM, (h + 1) * XA_HEAD_DIM)
        s = _nt_dot(q_ref[:, sl], km_ref[:, sl])
        m = jnp.max(s, axis=-1, keepdims=True)
        p = jnp.exp2(s - m)
        l = jnp.sum(p, axis=-1, keepdims=True)
        o = jnp.dot(p.astype(BF16), vm_ref[:, sl], preferred_element_type=F32) / l
        o_ref[:, sl] = (o * _silu(g_ref[:, sl].astype(F32))).astype(BF16)


def _memxattn(proj, kv, tq):
    b, s, _ = proj.shape
    m = kv.shape[1]
    return pl.pallas_call(
        _memxattn_kernel,
        grid=(b, s // tq),
        in_specs=[
            pl.BlockSpec((None, tq, XA_W), lambda bi, i: (bi, i, C_XA_Q)),
            pl.BlockSpec((None, tq, XA_W), lambda bi, i: (bi, i, C_XA_G)),
            pl.BlockSpec((None, m, XA_W), lambda bi, i: (bi, 0, 0)),
            pl.BlockSpec((None, m, XA_W), lambda bi, i: (bi, 0, 1)),
        ],
        out_specs=pl.BlockSpec((None, tq, XA_W), lambda bi, i: (bi, i, 0)),
        out_shape=jax.ShapeDtypeStruct((b, s, XA_W), BF16),
        compiler_params=_params(("arbitrary", "arbitrary")),
        name="memxattn",
    )(proj, proj, kv, kv)


def _outproj_kernel(yrg_ref, yda_ref, yxa_ref, w_ref, x_ref, g_ref, o_ref):
    y = jnp.dot(yrg_ref[...], w_ref[0:RG_W, :], preferred_element_type=F32)
    y = y + jnp.dot(yda_ref[...], w_ref[RG_W:RG_W + DA_W, :], preferred_element_type=F32)
    y = y + jnp.dot(yxa_ref[...], w_ref[RG_W + DA_W:, :], preferred_element_type=F32)
    o_ref[...] = x_ref[...] + _rms(y, g_ref[...])


def _outproj(y_rg, y_da, y_xa, w_bf, x, g_post, tm):
    b, s, d = x.shape
    d_mix = w_bf.shape[0]
    tile = lambda width: pl.BlockSpec((None, tm, width), lambda i, j: (i, j, 0))
    return pl.pallas_call(
        _outproj_kernel,
        grid=(b, s // tm),
        in_specs=[
            tile(RG_W), tile(DA_W), tile(XA_W),
            pl.BlockSpec((d_mix, d), lambda i, j: (0, 0)),
            tile(d),
            pl.BlockSpec((1, d), lambda i, j: (0, 0)),
        ],
        out_specs=tile(d),
        out_shape=jax.ShapeDtypeStruct((b, s, d), F32),
        compiler_params=_params(("arbitrary", "arbitrary")),
        name="outproj",
    )(y_rg, y_da, y_xa, w_bf, x, g_post.reshape(1, d))


def kernel(x, mem, g_pre, g_mem, w_in, w_mem_kv, conv_w, conv_b, w_rg_a, b_rg_a, w_rg_x, b_rg_x,
           lru_lambda, lambda_q1, lambda_k1, lambda_q2, lambda_k2, g_subln, w_out, g_post):
    kv = _memkv(mem, g_mem, w_mem_kv.astype(BF16))
    proj = _inproj(x, g_pre, w_in.astype(BF16), tm=512)
    y_rg = _rglru(proj, conv_w, conv_b, w_rg_a, b_rg_a, w_rg_x, b_rg_x, lru_lambda, ts=128)
    y_da = _diffattn(proj, lambda_q1, lambda_k1, lambda_q2, lambda_k2, g_subln, tq=256)
    y_xa = _memxattn(proj, kv, tq=512)
    return _outproj(y_rg, y_da, y_xa, w_out.astype(BF16), x, g_post, tm=512)
```

```python
import functools
import math

import jax
import jax.numpy as jnp
from jax import lax
from jax.experimental import pallas as pl
from jax.experimental.pallas import tpu as pltpu

F32 = jnp.float32
BF16 = jnp.bfloat16

RMS_EPS = 1e-6
N_MEM = 256
RG_HEADS = 8
RG_HEAD_DIM = 64
RG_W = RG_HEADS * RG_HEAD_DIM
CONV_WIDTH = 4
LRU_C = 8.0
DA_HEADS = 4
DA_HEAD_DIM = 64
DA_V_DIM = 2 * DA_HEAD_DIM
DA_W = DA_HEADS * DA_V_DIM
LAMBDA_INIT = 0.8 - 0.6 * math.exp(-0.3 * (1 - 1))
XA_HEADS = 4
XA_HEAD_DIM = 128
XA_W = XA_HEADS * XA_HEAD_DIM
LOG2E = 1.4426950408889634

PROJ_CHUNK = 512
(C_RG_X, C_RG_G, C_DA_Q, C_DA_K, C_DA_V, C_DA_G, C_XA_Q, C_XA_G) = range(8)
D_IN = 8 * PROJ_CHUNK

SUBLANES = 8
LANES = 128
BF16_ROWS = 16
MXU_DIM = 256
VMEM_LIMIT_BYTES = 56 * 1024 * 1024


def _params(sem):
    return pltpu.CompilerParams(dimension_semantics=sem, vmem_limit_bytes=VMEM_LIMIT_BYTES)


def _rms(x, g):
    ms = jnp.mean(x * x, axis=-1, keepdims=True)
    return x * lax.rsqrt(ms + RMS_EPS) * g


def _silu(x):
    h = 0.5 * x
    return h * (1.0 + jnp.tanh(h))


def _nt_dot(a, b):
    return lax.dot_general(a, b, (((1,), (1,)), ((), ())), preferred_element_type=F32)


def _memkv_kernel(mem_ref, g_ref, w_ref, km_ref, vmt_ref):
    mn = _rms(mem_ref[...], g_ref[...]).astype(BF16)
    kv = jnp.dot(mn, w_ref[...], preferred_element_type=F32)
    km_ref[...] = kv[:, 0:XA_W].astype(BF16)
    m = kv.shape[0]
    for h in range(XA_HEADS):
        lo = XA_W + h * XA_HEAD_DIM
        vmt_ref[h, 0:XA_HEAD_DIM, :] = kv[:, lo:lo + XA_HEAD_DIM].T.astype(BF16)
        vmt_ref[h, XA_HEAD_DIM:XA_HEAD_DIM + BF16_ROWS, :] = jnp.ones((BF16_ROWS, m), BF16)


def _memkv(mem, g_mem, w_bf):
    b, m, d = mem.shape
    vt_rows = XA_HEAD_DIM + BF16_ROWS
    return pl.pallas_call(
        _memkv_kernel,
        grid=(b,),
        in_specs=[
            pl.BlockSpec((None, m, d), lambda i: (i, 0, 0)),
            pl.BlockSpec((1, d), lambda i: (0, 0)),
            pl.BlockSpec((d, 2 * XA_W), lambda i: (0, 0)),
        ],
        out_specs=[pl.BlockSpec((None, m, XA_W), lambda i: (i, 0, 0)),
                   pl.BlockSpec((None, XA_HEADS, vt_rows, m), lambda i: (i, 0, 0, 0))],
        out_shape=[jax.ShapeDtypeStruct((b, m, XA_W), BF16),
                   jax.ShapeDtypeStruct((b, XA_HEADS, vt_rows, m), BF16)],
        compiler_params=_params(("arbitrary",)),
        name="memkv",
    )(mem, g_mem.reshape(1, d), w_bf)


def _inproj_kernel(x_ref, g_ref, w_ref, proj_ref):
    hn = _rms(x_ref[...], g_ref[...]).astype(BF16)
    q_scale = {C_DA_Q: (DA_HEAD_DIM ** -0.5) * LOG2E, C_XA_Q: (XA_HEAD_DIM ** -0.5) * LOG2E}
    for n in range(D_IN // PROJ_CHUNK):
        cols = slice(n * PROJ_CHUNK, (n + 1) * PROJ_CHUNK)
        acc = jnp.dot(hn, w_ref[:, cols], preferred_element_type=F32)
        if n in q_scale:
            acc = acc * q_scale[n]
        proj_ref[:, cols] = acc.astype(BF16)


def _inproj(x, g_pre, w_bf, tm):
    b, s, d = x.shape
    return pl.pallas_call(
        _inproj_kernel,
        grid=(b, s // tm),
        in_specs=[
            pl.BlockSpec((None, tm, d), lambda i, j: (i, j, 0)),
            pl.BlockSpec((1, d), lambda i, j: (0, 0)),
            pl.BlockSpec((d, D_IN), lambda i, j: (0, 0)),
        ],
        out_specs=pl.BlockSpec((None, tm, D_IN), lambda i, j: (i, j, 0)),
        out_shape=jax.ShapeDtypeStruct((b, s, D_IN), BF16),
        compiler_params=_params(("arbitrary", "arbitrary")),
        name="inproj",
    )(x, g_pre.reshape(1, d), w_bf)


RG_SUB = MXU_DIM // 8


def _rglru_kernel(p_ref, cw_ref, cb_ref, wa_ref, ba_ref, wx_ref, bx_ref, lam_ref,
                  y_ref, xp_ref, g_scr, a_ref, u_ref, h_ref, *, ts, nb):
    step = pl.program_id(0)
    halo = CONV_WIDTH - 1
    sub = RG_SUB
    rows = sub * nb
    assert rows == MXU_DIM and nb == SUBLANES and ts % sub == 0

    @pl.when(step == 0)
    def _():
        xp_ref[0:halo] = jnp.zeros((halo, nb, RG_W), F32)
        h_ref[...] = jnp.zeros((nb, RG_W), F32)

    @pl.when(step > 0)
    def _():
        xp_ref[0:halo] = xp_ref[ts:ts + halo]

    r = lax.broadcasted_iota(jnp.int32, (rows, rows), 0)
    c = lax.broadcasted_iota(jnp.int32, (rows, rows), 1)
    log_nb, log_sub = nb.bit_length() - 1, sub.bit_length() - 1
    to_time_major = (c == (r & (nb - 1)) * sub + (r >> log_nb)).astype(BF16)
    to_batch_major = (c == (r & (sub - 1)) * nb + (r >> log_sub)).astype(BF16)

    for sb in range(ts // sub):
        t0 = sb * sub
        blk = p_ref[:, t0:t0 + sub, :].reshape(rows, 2 * RG_W)
        tm = jnp.dot(to_time_major, blk, preferred_element_type=F32).reshape(sub, nb, 2 * RG_W)
        xp_ref[halo + t0:halo + t0 + sub] = tm[:, :, 0:RG_W]
        g_scr[t0:t0 + sub] = tm[:, :, RG_W:2 * RG_W]

    xh = 0.5 * cb_ref[...].reshape(1, 1, RG_W)
    for k in range(CONV_WIDTH):
        xh = xh + (0.5 * cw_ref[k:k + 1, :]).reshape(1, 1, RG_W) * xp_ref[k:k + ts]

    xb = xh.reshape(ts * nb, RG_W).astype(BF16)

    def gate(w_ref, b_ref):
        z = [jnp.dot(xb[:, m * MXU_DIM:(m + 1) * MXU_DIM], w_ref[m], preferred_element_type=F32)
             for m in range(RG_W // MXU_DIM)]
        return jnp.tanh(jnp.concatenate(z, axis=1) + 0.5 * b_ref[...]).reshape(ts, nb, RG_W)

    tr = gate(wa_ref, ba_ref)
    ti = gate(wx_ref, bx_ref)

    z = -lam_ref[...]
    softplus = jnp.maximum(z, 0.0) + jnp.log1p(jnp.exp(-jnp.abs(z)))
    half_rate = ((-0.5 * LRU_C * LOG2E) * softplus).reshape(1, 1, RG_W)
    a = jnp.exp2(tr * half_rate + half_rate)
    d = jnp.maximum(1.0 - a * a, 0.0)
    mult = jnp.where(d > 0.0, d * lax.rsqrt(d), 0.0)
    a_ref[...] = a
    u_ref[...] = mult * (xh * (1.0 + ti))

    def body(t, h):
        h = a_ref[t] * h + u_ref[t]
        u_ref[t] = h
        return h

    h_ref[...] = lax.fori_loop(0, ts, body, h_ref[...], unroll=8)

    for sb in range(ts // sub):
        t0 = sb * sub
        y = (u_ref[t0:t0 + sub] * _silu(g_scr[t0:t0 + sub])).reshape(rows, RG_W).astype(BF16)
        bm = jnp.dot(to_batch_major, y, preferred_element_type=F32)
        y_ref[:, t0:t0 + sub, :] = bm.reshape(nb, sub, RG_W).astype(BF16)


def _block_diag(w, group):
    h, di, dj = w.shape
    w = w.reshape(h // group, group, di, dj)
    eye = jnp.eye(group, dtype=w.dtype)
    return (eye[None, :, None, :, None] * w[:, :, :, None, :]).reshape(h // group, group * di, group * dj)


def _rglru(proj, conv_w, conv_b, w_rg_a, b_rg_a, w_rg_x, b_rg_x, lru_lambda, ts):
    nb, s, _ = proj.shape
    group = MXU_DIM // RG_HEAD_DIM
    n_blk = RG_HEADS // group
    wa = _block_diag(w_rg_a, group).astype(BF16)
    wx = _block_diag(w_rg_x, group).astype(BF16)
    row = lambda v: v.reshape(1, RG_W)
    const2 = lambda shape: pl.BlockSpec(shape, lambda i: (0, 0))
    const3 = lambda shape: pl.BlockSpec(shape, lambda i: (0, 0, 0))
    return pl.pallas_call(
        functools.partial(_rglru_kernel, ts=ts, nb=nb),
        grid=(s // ts,),
        in_specs=[pl.BlockSpec((nb, ts, 2 * RG_W), lambda i: (0, i, 0)),
                  const2((CONV_WIDTH, RG_W)), const2((1, RG_W)),
                  const3((n_blk, MXU_DIM, MXU_DIM)), const2((1, RG_W)),
                  const3((n_blk, MXU_DIM, MXU_DIM)), const2((1, RG_W)),
                  const2((1, RG_W))],
        out_specs=pl.BlockSpec((nb, ts, RG_W), lambda i: (0, i, 0)),
        out_shape=jax.ShapeDtypeStruct((nb, s, RG_W), BF16),
        scratch_shapes=[
            pltpu.VMEM((ts + CONV_WIDTH - 1, nb, RG_W), F32),
            pltpu.VMEM((ts, nb, RG_W), F32),
            pltpu.VMEM((ts, nb, RG_W), F32),
            pltpu.VMEM((ts, nb, RG_W), F32),
            pltpu.VMEM((nb, RG_W), F32),
        ],
        compiler_params=_params(("arbitrary",)),
        name="rglru",
    )(proj, conv_w, row(conv_b), wa, row(b_rg_a), wx, row(b_rg_x), row(lru_lambda))


def _diffattn_kernel(q_ref, k_ref, v_ref, g_ref, lq1_ref, lk1_ref, lq2_ref, lk2_ref, gs_ref,
                     o_ref, vt_ref, *, tq):
    hd = DA_V_DIM
    s_len = q_ref.shape[0]
    nq = s_len // tq
    w = 2 * tq

    vt_ref[0:hd, :] = v_ref[...].astype(F32).T.astype(BF16)
    vt_ref[hd:hd + BF16_ROWS, :] = jnp.ones((BF16_ROWS, s_len), BF16)

    lam = (jnp.exp(jnp.sum(lq1_ref[...] * lk1_ref[...], axis=-1, keepdims=True))
           - jnp.exp(jnp.sum(lq2_ref[...] * lk2_ref[...], axis=-1, keepdims=True))
           + LAMBDA_INIT)
    lane = lax.broadcasted_iota(jnp.int32, (tq, hd), 1)
    key = lax.broadcasted_iota(jnp.int32, (tq, w), 0)
    qry = lax.broadcasted_iota(jnp.int32, (tq, w), 1) & (tq - 1)
    causal = key <= qry

    def scores(qb):
        kv = (qb + 1) * tq
        q = q_ref[qb * tq:kv, :].astype(F32)
        qq = jnp.concatenate([jnp.where(lane < DA_HEAD_DIM, q, 0.0),
                              jnp.where(lane >= DA_HEAD_DIM, q, 0.0)], axis=0)
        qqt = qq.T.astype(BF16)
        s_diag = jnp.dot(k_ref[kv - tq:kv, :], qqt, preferred_element_type=F32)
        parts = [jnp.where(causal, s_diag, -jnp.inf)]
        if qb > 0:
            parts.insert(0, jnp.dot(k_ref[0:kv - tq, :], qqt, preferred_element_type=F32))
        return jnp.concatenate(parts, axis=0).reshape(kv // SUBLANES, SUBLANES, w)

    order = [0] + list(range(nq - 1, 0, -1))
    ahead = 2
    pending = [scores(qb) for qb in order[:ahead]]
    for pos, qb in enumerate(order):
        rows = slice(qb * tq, (qb + 1) * tq)
        kv = (qb + 1) * tq
        s3 = pending.pop(0)
        if pos + ahead < nq:
            pending.append(scores(order[pos + ahead]))
        m = jnp.max(jnp.max(s3, axis=0), axis=0, keepdims=True)
        p = jnp.exp2(s3 - m[None]).reshape(kv, w).astype(BF16)
        pv = jnp.dot(vt_ref[:, 0:kv], p, preferred_element_type=F32)
        ot = pv[0:hd] / pv[hd:hd + 1]
        od = (ot[:, 0:tq] - lam * ot[:, tq:w]).T
        o = _rms(od, gs_ref[...]) * (1.0 - LAMBDA_INIT)
        o_ref[rows, :] = (o * _silu(g_ref[rows, :].astype(F32))).astype(BF16)


def _diffattn(proj, lq1, lk1, lq2, lk2, g_subln, tq):
    b, s, _ = proj.shape
    hd = DA_V_DIM
    per_chunk = PROJ_CHUNK // hd
    vec = lambda v: v.reshape(1, -1)
    cvec = lambda n: pl.BlockSpec((1, n), lambda bi, h: (0, 0))
    col = lambda chunk: pl.BlockSpec((None, s, hd), lambda bi, h: (bi, 0, chunk * per_chunk + h))
    return pl.pallas_call(
        functools.partial(_diffattn_kernel, tq=tq),
        grid=(b, DA_HEADS),
        in_specs=[col(C_DA_Q), col(C_DA_K), col(C_DA_V), col(C_DA_G),
                  cvec(DA_HEAD_DIM), cvec(DA_HEAD_DIM), cvec(DA_HEAD_DIM), cvec(DA_HEAD_DIM),
                  cvec(DA_V_DIM)],
        out_specs=pl.BlockSpec((None, s, hd), lambda bi, h: (bi, 0, h)),
        out_shape=jax.ShapeDtypeStruct((b, s, DA_W), BF16),
        scratch_shapes=[pltpu.VMEM((hd + BF16_ROWS, s), BF16)],
        compiler_params=_params(("arbitrary", "arbitrary")),
        name="diffattn",
    )(proj, proj, proj, proj, vec(lq1), vec(lk1), vec(lq2), vec(lk2), vec(g_subln))


def _memxattn_kernel(q_ref, g_ref, km_ref, vmt_ref, o_ref):
    dh = XA_HEAD_DIM
    n_mem = km_ref.shape[0]
    tq = q_ref.shape[0]
    heads = [slice(h * dh, (h + 1) * dh) for h in range(XA_HEADS)]
    scores = [_nt_dot(km_ref[:, sl], q_ref[:, sl]).reshape(n_mem // SUBLANES, SUBLANES, tq)
              for sl in heads]
    for h, sl in enumerate(heads):
        s3 = scores[h]
        m = jnp.max(jnp.max(s3, axis=0), axis=0, keepdims=True)
        p = jnp.exp2(s3 - m[None]).reshape(n_mem, tq).astype(BF16)
        pv = jnp.dot(vmt_ref[h], p, preferred_element_type=F32)
        o = (pv[0:dh] / pv[dh:dh + 1]).T
        o_ref[:, sl] = (o * _silu(g_ref[:, sl].astype(F32))).astype(BF16)


def _memxattn(proj, km, vmt, tq):
    b, s, _ = proj.shape
    m = km.shape[1]
    return pl.pallas_call(
        _memxattn_kernel,
        grid=(b, s // tq),
        in_specs=[
            pl.BlockSpec((None, tq, XA_W), lambda bi, i: (bi, i, C_XA_Q)),
            pl.BlockSpec((None, tq, XA_W), lambda bi, i: (bi, i, C_XA_G)),
            pl.BlockSpec((None, m, XA_W), lambda bi, i: (bi, 0, 0)),
            pl.BlockSpec((None,) + vmt.shape[1:], lambda bi, i: (bi, 0, 0, 0)),
        ],
        out_specs=pl.BlockSpec((None, tq, XA_W), lambda bi, i: (bi, i, 0)),
        out_shape=jax.ShapeDtypeStruct((b, s, XA_W), BF16),
        compiler_params=_params(("arbitrary", "arbitrary")),
        name="memxattn",
    )(proj, proj, km, vmt)


def _outproj_kernel(yrg_ref, yda_ref, yxa_ref, w_ref, x_ref, g_ref, o_ref):
    y = jnp.dot(yrg_ref[...], w_ref[0:RG_W, :], preferred_element_type=F32)
    y = y + jnp.dot(yda_ref[...], w_ref[RG_W:RG_W + DA_W, :], preferred_element_type=F32)
    y = y + jnp.dot(yxa_ref[...], w_ref[RG_W + DA_W:, :], preferred_element_type=F32)
    o_ref[...] = x_ref[...] + _rms(y, g_ref[...])


def _outproj(y_rg, y_da, y_xa, w_bf, x, g_post, tm):
    b, s, d = x.shape
    d_mix = w_bf.shape[0]
    tile = lambda width: pl.BlockSpec((None, tm, width), lambda i, j: (i, j, 0))
    return pl.pallas_call(
        _outproj_kernel,
        grid=(b, s // tm),
        in_specs=[
            tile(RG_W), tile(DA_W), tile(XA_W),
            pl.BlockSpec((d_mix, d), lambda i, j: (0, 0)),
            tile(d),
            pl.BlockSpec((1, d), lambda i, j: (0, 0)),
        ],
        out_specs=tile(d),
        out_shape=jax.ShapeDtypeStruct((b, s, d), F32),
        compiler_params=_params(("arbitrary", "arbitrary")),
        name="outproj",
    )(y_rg, y_da, y_xa, w_bf, x, g_post.reshape(1, d))


def kernel(x, mem, g_pre, g_mem, w_in, w_mem_kv, conv_w, conv_b, w_rg_a, b_rg_a, w_rg_x, b_rg_x,
           lru_lambda, lambda_q1, lambda_k1, lambda_q2, lambda_k2, g_subln, w_out, g_post):
    km, vmt = _memkv(mem, g_mem, w_mem_kv.astype(BF16))
    proj = _inproj(x, g_pre, w_in.astype(BF16), tm=1024)
    y_rg = _rglru(proj, conv_w, conv_b, w_rg_a, b_rg_a, w_rg_x, b_rg_x, lru_lambda, ts=128)
    y_da = _diffattn(proj, lambda_q1, lambda_k1, lambda_q2, lambda_k2, g_subln, tq=256)
    y_xa = _memxattn(proj, km, vmt, tq=1024)
    return _outproj(y_rg, y_da, y_xa, w_out.astype(BF16), x, g_post, tm=1024)
```

```python
import functools
import math

import jax
import jax.numpy as jnp
from jax import lax
from jax.experimental import pallas as pl
from jax.experimental.pallas import tpu as pltpu

F32 = jnp.float32
BF16 = jnp.bfloat16

RMS_EPS = 1e-6
N_MEM = 256
RG_HEADS = 8
RG_HEAD_DIM = 64
RG_W = RG_HEADS * RG_HEAD_DIM
CONV_WIDTH = 4
LRU_C = 8.0
DA_HEADS = 4
DA_HEAD_DIM = 64
DA_V_DIM = 2 * DA_HEAD_DIM
DA_W = DA_HEADS * DA_V_DIM
LAMBDA_INIT = 0.8 - 0.6 * math.exp(-0.3 * (1 - 1))
XA_HEADS = 4
XA_HEAD_DIM = 128
XA_W = XA_HEADS * XA_HEAD_DIM
LOG2E = 1.4426950408889634

PROJ_CHUNK = 512
(C_RG_X, C_RG_G, C_DA_Q, C_DA_K, C_DA_V, C_DA_G, C_XA_Q, C_XA_G) = range(8)
D_IN = 8 * PROJ_CHUNK

SUBLANES = 8
LANES = 128
BF16_ROWS = 16
MXU_DIM = 256
VMEM_LIMIT_BYTES = 56 * 1024 * 1024


def _params(sem):
    return pltpu.CompilerParams(dimension_semantics=sem, vmem_limit_bytes=VMEM_LIMIT_BYTES)


def _rms(x, g):
    ms = jnp.mean(x * x, axis=-1, keepdims=True)
    return x * lax.rsqrt(ms + RMS_EPS) * g


def _silu(x):
    h = 0.5 * x
    return h * (1.0 + jnp.tanh(h))


def _nt_dot(a, b):
    return lax.dot_general(a, b, (((1,), (1,)), ((), ())), preferred_element_type=F32)


def _inproj_kernel(x_ref, g_ref, w_ref, proj_ref):
    hn = _rms(x_ref[...], g_ref[...]).astype(BF16)
    q_scale = {C_DA_Q: (DA_HEAD_DIM ** -0.5) * LOG2E, C_XA_Q: (XA_HEAD_DIM ** -0.5) * LOG2E}
    for n in range(D_IN // PROJ_CHUNK):
        cols = slice(n * PROJ_CHUNK, (n + 1) * PROJ_CHUNK)
        acc = jnp.dot(hn, w_ref[:, cols], preferred_element_type=F32)
        if n in q_scale:
            acc = acc * q_scale[n]
        proj_ref[:, cols] = acc.astype(BF16)


def _inproj(x, g_pre, w_bf, tm):
    b, s, d = x.shape
    return pl.pallas_call(
        _inproj_kernel,
        grid=(b, s // tm),
        in_specs=[
            pl.BlockSpec((None, tm, d), lambda i, j: (i, j, 0)),
            pl.BlockSpec((1, d), lambda i, j: (0, 0)),
            pl.BlockSpec((d, D_IN), lambda i, j: (0, 0)),
        ],
        out_specs=pl.BlockSpec((None, tm, D_IN), lambda i, j: (i, j, 0)),
        out_shape=jax.ShapeDtypeStruct((b, s, D_IN), BF16),
        compiler_params=_params(("arbitrary", "arbitrary")),
        name="inproj",
    )(x, g_pre.reshape(1, d), w_bf)


RG_SUB = MXU_DIM // 8


def _rglru_kernel(p_ref, cw_ref, cb_ref, wa_ref, ba_ref, wx_ref, bx_ref, lam_ref,
                  y_ref, xp_ref, g_scr, a_ref, u_ref, h_ref, *, ts, nb):
    step = pl.program_id(0)
    halo = CONV_WIDTH - 1
    sub = RG_SUB
    rows = sub * nb
    assert rows == MXU_DIM and nb == SUBLANES and ts % sub == 0

    @pl.when(step == 0)
    def _():
        xp_ref[0:halo] = jnp.zeros((halo, nb, RG_W), F32)
        h_ref[...] = jnp.zeros((nb, RG_W), F32)

    @pl.when(step > 0)
    def _():
        xp_ref[0:halo] = xp_ref[ts:ts + halo]

    r = lax.broadcasted_iota(jnp.int32, (rows, rows), 0)
    c = lax.broadcasted_iota(jnp.int32, (rows, rows), 1)
    log_nb, log_sub = nb.bit_length() - 1, sub.bit_length() - 1
    to_time_major = (c == (r & (nb - 1)) * sub + (r >> log_nb)).astype(BF16)
    to_batch_major = (c == (r & (sub - 1)) * nb + (r >> log_sub)).astype(BF16)

    for sb in range(ts // sub):
        t0 = sb * sub
        blk = p_ref[:, t0:t0 + sub, :].reshape(rows, 2 * RG_W)
        tm = jnp.dot(to_time_major, blk, preferred_element_type=F32).reshape(sub, nb, 2 * RG_W)
        xp_ref[halo + t0:halo + t0 + sub] = tm[:, :, 0:RG_W]
        g_scr[t0:t0 + sub] = tm[:, :, RG_W:2 * RG_W]

    xh = 0.5 * cb_ref[...].reshape(1, 1, RG_W)
    for k in range(CONV_WIDTH):
        xh = xh + (0.5 * cw_ref[k:k + 1, :]).reshape(1, 1, RG_W) * xp_ref[k:k + ts]

    xb = xh.reshape(ts * nb, RG_W).astype(BF16)

    def gate(w_ref, b_ref):
        z = [jnp.dot(xb[:, m * MXU_DIM:(m + 1) * MXU_DIM], w_ref[m], preferred_element_type=F32)
             for m in range(RG_W // MXU_DIM)]
        return jnp.tanh(jnp.concatenate(z, axis=1) + 0.5 * b_ref[...]).reshape(ts, nb, RG_W)

    tr = gate(wa_ref, ba_ref)
    ti = gate(wx_ref, bx_ref)

    z = -lam_ref[...]
    softplus = jnp.maximum(z, 0.0) + jnp.log1p(jnp.exp(-jnp.abs(z)))
    half_rate = ((-0.5 * LRU_C * LOG2E) * softplus).reshape(1, 1, RG_W)
    a = jnp.exp2(tr * half_rate + half_rate)
    d = jnp.maximum(1.0 - a * a, 0.0)
    mult = jnp.where(d > 0.0, d * lax.rsqrt(d), 0.0)
    a_ref[...] = a
    u_ref[...] = mult * (xh * (1.0 + ti))

    def body(t, h):
        h = a_ref[t] * h + u_ref[t]
        u_ref[t] = h
        return h

    h_ref[...] = lax.fori_loop(0, ts, body, h_ref[...], unroll=8)

    for sb in range(ts // sub):
        t0 = sb * sub
        y = (u_ref[t0:t0 + sub] * _silu(g_scr[t0:t0 + sub])).reshape(rows, RG_W).astype(BF16)
        bm = jnp.dot(to_batch_major, y, preferred_element_type=F32)
        y_ref[:, t0:t0 + sub, :] = bm.reshape(nb, sub, RG_W).astype(BF16)


def _block_diag(w, group):
    h, di, dj = w.shape
    w = w.reshape(h // group, group, di, dj)
    eye = jnp.eye(group, dtype=w.dtype)
    return (eye[None, :, None, :, None] * w[:, :, :, None, :]).reshape(h // group, group * di, group * dj)


def _rglru(proj, conv_w, conv_b, w_rg_a, b_rg_a, w_rg_x, b_rg_x, lru_lambda, ts):
    nb, s, _ = proj.shape
    group = MXU_DIM // RG_HEAD_DIM
    n_blk = RG_HEADS // group
    wa = _block_diag(w_rg_a, group).astype(BF16)
    wx = _block_diag(w_rg_x, group).astype(BF16)
    row = lambda v: v.reshape(1, RG_W)
    const2 = lambda shape: pl.BlockSpec(shape, lambda i: (0, 0))
    const3 = lambda shape: pl.BlockSpec(shape, lambda i: (0, 0, 0))
    return pl.pallas_call(
        functools.partial(_rglru_kernel, ts=ts, nb=nb),
        grid=(s // ts,),
        in_specs=[pl.BlockSpec((nb, ts, 2 * RG_W), lambda i: (0, i, 0)),
                  const2((CONV_WIDTH, RG_W)), const2((1, RG_W)),
                  const3((n_blk, MXU_DIM, MXU_DIM)), const2((1, RG_W)),
                  const3((n_blk, MXU_DIM, MXU_DIM)), const2((1, RG_W)),
                  const2((1, RG_W))],
        out_specs=pl.BlockSpec((nb, ts, RG_W), lambda i: (0, i, 0)),
        out_shape=jax.ShapeDtypeStruct((nb, s, RG_W), BF16),
        scratch_shapes=[
            pltpu.VMEM((ts + CONV_WIDTH - 1, nb, RG_W), F32),
            pltpu.VMEM((ts, nb, RG_W), F32),
            pltpu.VMEM((ts, nb, RG_W), F32),
            pltpu.VMEM((ts, nb, RG_W), F32),
            pltpu.VMEM((nb, RG_W), F32),
        ],
        compiler_params=_params(("arbitrary",)),
        name="rglru",
    )(proj, conv_w, row(conv_b), wa, row(b_rg_a), wx, row(b_rg_x), row(lru_lambda))


def _attn_kernel(q_ref, k_ref, v_ref, g_ref, xq_ref, xg_ref, mem_ref, gm_ref, wm_ref,
                 lq1_ref, lk1_ref, lq2_ref, lk2_ref, gs_ref,
                 yda_ref, yxa_ref, vt_ref, km_ref, vmt_ref, *, tq):
    h = pl.program_id(1)
    hd = DA_V_DIM
    dh = XA_HEAD_DIM
    s_len = q_ref.shape[0]
    n_mem = mem_ref.shape[0]
    nq = s_len // tq
    w = 2 * tq

    @pl.when(h == 0)
    def _():
        mn = _rms(mem_ref[...], gm_ref[...]).astype(BF16)
        kv = jnp.dot(mn, wm_ref[...], preferred_element_type=F32)
        for hh in range(XA_HEADS):
            km_ref[hh] = kv[:, hh * dh:(hh + 1) * dh].astype(BF16)
            lo = XA_W + hh * dh
            vmt_ref[hh, 0:dh, :] = kv[:, lo:lo + dh].T.astype(BF16)
            vmt_ref[hh, dh:dh + BF16_ROWS, :] = jnp.ones((BF16_ROWS, n_mem), BF16)

    vt_ref[0:hd, :] = v_ref[...].astype(F32).T.astype(BF16)
    vt_ref[hd:hd + BF16_ROWS, :] = jnp.ones((BF16_ROWS, s_len), BF16)

    lam = (jnp.exp(jnp.sum(lq1_ref[...] * lk1_ref[...], axis=-1, keepdims=True))
           - jnp.exp(jnp.sum(lq2_ref[...] * lk2_ref[...], axis=-1, keepdims=True))
           + LAMBDA_INIT)
    lane = lax.broadcasted_iota(jnp.int32, (tq, hd), 1)
    key = lax.broadcasted_iota(jnp.int32, (tq, w), 0)
    qry = lax.broadcasted_iota(jnp.int32, (tq, w), 1) & (tq - 1)
    causal = key <= qry

    def softmax_t(s3, vt):
        n = s3.shape[0] * SUBLANES
        m = jnp.max(jnp.max(s3, axis=0), axis=0, keepdims=True)
        p = jnp.exp2(s3 - m[None]).reshape(n, s3.shape[2]).astype(BF16)
        return jnp.dot(vt, p, preferred_element_type=F32)

    def da_scores(qb):
        kv = (qb + 1) * tq
        q = q_ref[qb * tq:kv, :].astype(F32)
        qq = jnp.concatenate([jnp.where(lane < DA_HEAD_DIM, q, 0.0),
                              jnp.where(lane >= DA_HEAD_DIM, q, 0.0)], axis=0)
        qqt = qq.T.astype(BF16)
        s_diag = jnp.dot(k_ref[kv - tq:kv, :], qqt, preferred_element_type=F32)
        parts = [jnp.where(causal, s_diag, -jnp.inf)]
        if qb > 0:
            parts.insert(0, jnp.dot(k_ref[0:kv - tq, :], qqt, preferred_element_type=F32))
        return jnp.concatenate(parts, axis=0).reshape(kv // SUBLANES, SUBLANES, w)

    def da_finish(qb, s3):
        rows = slice(qb * tq, (qb + 1) * tq)
        pv = softmax_t(s3, vt_ref[:, 0:(qb + 1) * tq])
        ot = pv[0:hd] / pv[hd:hd + 1]
        od = (ot[:, 0:tq] - lam * ot[:, tq:w]).T
        o = _rms(od, gs_ref[...]) * (1.0 - LAMBDA_INIT)
        yda_ref[rows, :] = (o * _silu(g_ref[rows, :].astype(F32))).astype(BF16)

    def xa_scores(_):
        return _nt_dot(km_ref[h], xq_ref[...]).reshape(n_mem // SUBLANES, SUBLANES, s_len)

    def xa_finish(_, s3):
        pv = softmax_t(s3, vmt_ref[h])
        o = (pv[0:dh] / pv[dh:dh + 1]).T
        yxa_ref[...] = (o * _silu(xg_ref[...].astype(F32))).astype(BF16)

    chains = [(da_scores, da_finish, 0), (xa_scores, xa_finish, None)]
    chains += [(da_scores, da_finish, qb) for qb in range(nq - 1, 0, -1)]
    ahead = 2
    pending = [sc(arg) for sc, _, arg in chains[:ahead]]
    for pos, (_, finish, arg) in enumerate(chains):
        s3 = pending.pop(0)
        if pos + ahead < len(chains):
            sc, _, nxt = chains[pos + ahead]
            pending.append(sc(nxt))
        finish(arg, s3)


def _attn(proj, mem, g_mem, wm_bf, lq1, lk1, lq2, lk2, g_subln, tq):
    b, s, _ = proj.shape
    m, d = mem.shape[1:]
    hd = DA_V_DIM
    assert XA_HEAD_DIM == hd and XA_HEADS == DA_HEADS
    per_chunk = PROJ_CHUNK // hd
    vec = lambda v: v.reshape(1, -1)
    cvec = lambda n: pl.BlockSpec((1, n), lambda bi, h: (0, 0))
    col = lambda chunk: pl.BlockSpec((None, s, hd), lambda bi, h: (bi, 0, chunk * per_chunk + h))
    out = pl.BlockSpec((None, s, hd), lambda bi, h: (bi, 0, h))
    return pl.pallas_call(
        functools.partial(_attn_kernel, tq=tq),
        grid=(b, DA_HEADS),
        in_specs=[col(C_DA_Q), col(C_DA_K), col(C_DA_V), col(C_DA_G), col(C_XA_Q), col(C_XA_G),
                  pl.BlockSpec((None, m, d), lambda bi, h: (bi, 0, 0)),
                  cvec(d),
                  pl.BlockSpec((d, 2 * XA_W), lambda bi, h: (0, 0)),
                  cvec(DA_HEAD_DIM), cvec(DA_HEAD_DIM), cvec(DA_HEAD_DIM), cvec(DA_HEAD_DIM),
                  cvec(DA_V_DIM)],
        out_specs=[out, out],
        out_shape=[jax.ShapeDtypeStruct((b, s, DA_W), BF16), jax.ShapeDtypeStruct((b, s, XA_W), BF16)],
        scratch_shapes=[pltpu.VMEM((hd + BF16_ROWS, s), BF16),
                        pltpu.VMEM((XA_HEADS, m, hd), BF16),
                        pltpu.VMEM((XA_HEADS, hd + BF16_ROWS, m), BF16)],
        compiler_params=_params(("arbitrary", "arbitrary")),
        name="attn",
    )(proj, proj, proj, proj, proj, proj, mem, vec(g_mem), wm_bf,
      vec(lq1), vec(lk1), vec(lq2), vec(lk2), vec(g_subln))


def _outproj_kernel(yrg_ref, yda_ref, yxa_ref, w_ref, x_ref, g_ref, o_ref):
    y = jnp.dot(yrg_ref[...], w_ref[0:RG_W, :], preferred_element_type=F32)
    y = y + jnp.dot(yda_ref[...], w_ref[RG_W:RG_W + DA_W, :], preferred_element_type=F32)
    y = y + jnp.dot(yxa_ref[...], w_ref[RG_W + DA_W:, :], preferred_element_type=F32)
    o_ref[...] = x_ref[...] + _rms(y, g_ref[...])


def _outproj(y_rg, y_da, y_xa, w_bf, x, g_post, tm):
    b, s, d = x.shape
    d_mix = w_bf.shape[0]
    tile = lambda width: pl.BlockSpec((None, tm, width), lambda i, j: (i, j, 0))
    return pl.pallas_call(
        _outproj_kernel,
        grid=(b, s // tm),
        in_specs=[
            tile(RG_W), tile(DA_W), tile(XA_W),
            pl.BlockSpec((d_mix, d), lambda i, j: (0, 0)),
            tile(d),
            pl.BlockSpec((1, d), lambda i, j: (0, 0)),
        ],
        out_specs=tile(d),
        out_shape=jax.ShapeDtypeStruct((b, s, d), F32),
        compiler_params=_params(("arbitrary", "arbitrary")),
        name="outproj",
    )(y_rg, y_da, y_xa, w_bf, x, g_post.reshape(1, d))


def kernel(x, mem, g_pre, g_mem, w_in, w_mem_kv, conv_w, conv_b, w_rg_a, b_rg_a, w_rg_x, b_rg_x,
           lru_lambda, lambda_q1, lambda_k1, lambda_q2, lambda_k2, g_subln, w_out, g_post):
    proj = _inproj(x, g_pre, w_in.astype(BF16), tm=1024)
    y_rg = _rglru(proj, conv_w, conv_b, w_rg_a, b_rg_a, w_rg_x, b_rg_x, lru_lambda, ts=128)
    y_da, y_xa = _attn(proj, mem, g_mem, w_mem_kv.astype(BF16),
                       lambda_q1, lambda_k1, lambda_q2, lambda_k2, g_subln, tq=256)
    return _outproj(y_rg, y_da, y_xa, w_out.astype(BF16), x, g_post, tm=1024)
```

```python
import functools
import math

import jax
import jax.numpy as jnp
from jax import lax
from jax.experimental import pallas as pl
from jax.experimental.pallas import tpu as pltpu

F32 = jnp.float32
BF16 = jnp.bfloat16

RMS_EPS = 1e-6
N_MEM = 256
RG_HEADS = 8
RG_HEAD_DIM = 64
RG_W = RG_HEADS * RG_HEAD_DIM
CONV_WIDTH = 4
LRU_C = 8.0
DA_HEADS = 4
DA_HEAD_DIM = 64
DA_V_DIM = 2 * DA_HEAD_DIM
DA_W = DA_HEADS * DA_V_DIM
LAMBDA_INIT = 0.8 - 0.6 * math.exp(-0.3 * (1 - 1))
XA_HEADS = 4
XA_HEAD_DIM = 128
XA_W = XA_HEADS * XA_HEAD_DIM
LOG2E = 1.4426950408889634

PROJ_CHUNK = 512
(C_RG_X, C_RG_G, C_DA_Q, C_DA_K, C_DA_V, C_DA_G, C_XA_Q, C_XA_G) = range(8)
D_IN = 8 * PROJ_CHUNK

SUBLANES = 8
LANES = 128
BF16_ROWS = 16
MXU_DIM = 256
VMEM_LIMIT_BYTES = 56 * 1024 * 1024


def _resident(shape, index_map):
    return pl.BlockSpec(shape, index_map, pipeline_mode=pl.Buffered(1))


def _params(sem):
    return pltpu.CompilerParams(dimension_semantics=sem, vmem_limit_bytes=VMEM_LIMIT_BYTES)


def _rms(x, g):
    ms = jnp.mean(x * x, axis=-1, keepdims=True)
    return x * lax.rsqrt(ms + RMS_EPS) * g


def _silu(x):
    h = 0.5 * x
    return h * (1.0 + jnp.tanh(h))


def _nt_dot(a, b):
    return lax.dot_general(a, b, (((1,), (1,)), ((), ())), preferred_element_type=F32)


def _inproj_kernel(x_ref, g_ref, w_ref, proj_ref):
    hn = _rms(x_ref[...], g_ref[...]).astype(BF16)
    q_scale = {C_DA_Q: (DA_HEAD_DIM ** -0.5) * LOG2E, C_XA_Q: (XA_HEAD_DIM ** -0.5) * LOG2E}
    for n in range(D_IN // PROJ_CHUNK):
        cols = slice(n * PROJ_CHUNK, (n + 1) * PROJ_CHUNK)
        acc = jnp.dot(hn, w_ref[:, cols].astype(BF16), preferred_element_type=F32)
        if n in q_scale:
            acc = acc * q_scale[n]
        proj_ref[:, cols] = acc.astype(BF16)


def _inproj(x, g_pre, w_in, tm):
    b, s, d = x.shape
    return pl.pallas_call(
        _inproj_kernel,
        grid=(b, s // tm),
        in_specs=[
            pl.BlockSpec((None, tm, d), lambda i, j: (i, j, 0)),
            pl.BlockSpec((1, d), lambda i, j: (0, 0)),
            _resident((d, D_IN), lambda i, j: (0, 0)),
        ],
        out_specs=pl.BlockSpec((None, tm, D_IN), lambda i, j: (i, j, 0)),
        out_shape=jax.ShapeDtypeStruct((b, s, D_IN), BF16),
        compiler_params=_params(("arbitrary", "arbitrary")),
        name="inproj",
    )(x, g_pre.reshape(1, d), w_in)


RG_SUB = MXU_DIM // 8


def _rglru_kernel(p_ref, cw_ref, cb_ref, wa_ref, ba_ref, wx_ref, bx_ref, lam_ref,
                  y_ref, xp_ref, g_scr, a_ref, u_ref, h_ref, *, ts, nb):
    step = pl.program_id(0)
    halo = CONV_WIDTH - 1
    sub = RG_SUB
    rows = sub * nb
    assert rows == MXU_DIM and nb == SUBLANES and ts % sub == 0

    @pl.when(step == 0)
    def _():
        xp_ref[0:halo] = jnp.zeros((halo, nb, RG_W), F32)
        h_ref[...] = jnp.zeros((nb, RG_W), F32)

    @pl.when(step > 0)
    def _():
        xp_ref[0:halo] = xp_ref[ts:ts + halo]

    r = lax.broadcasted_iota(jnp.int32, (rows, rows), 0)
    c = lax.broadcasted_iota(jnp.int32, (rows, rows), 1)
    log_nb, log_sub = nb.bit_length() - 1, sub.bit_length() - 1
    to_time_major = (c == (r & (nb - 1)) * sub + (r >> log_nb)).astype(BF16)
    to_batch_major = (c == (r & (sub - 1)) * nb + (r >> log_sub)).astype(BF16)

    for sb in range(ts // sub):
        t0 = sb * sub
        blk = p_ref[:, t0:t0 + sub, :].reshape(rows, 2 * RG_W)
        tm = jnp.dot(to_time_major, blk, preferred_element_type=F32).reshape(sub, nb, 2 * RG_W)
        xp_ref[halo + t0:halo + t0 + sub] = tm[:, :, 0:RG_W]
        g_scr[t0:t0 + sub] = tm[:, :, RG_W:2 * RG_W]

    xh = 0.5 * cb_ref[...].reshape(1, 1, RG_W)
    for k in range(CONV_WIDTH):
        xh = xh + (0.5 * cw_ref[k:k + 1, :]).reshape(1, 1, RG_W) * xp_ref[k:k + ts]

    xb = xh.reshape(ts * nb, RG_W).astype(BF16)

    def gate(w_ref, b_ref):
        z = [jnp.dot(xb[:, m * MXU_DIM:(m + 1) * MXU_DIM], w_ref[m], preferred_element_type=F32)
             for m in range(RG_W // MXU_DIM)]
        return jnp.tanh(jnp.concatenate(z, axis=1) + 0.5 * b_ref[...]).reshape(ts, nb, RG_W)

    tr = gate(wa_ref, ba_ref)
    ti = gate(wx_ref, bx_ref)

    z = -lam_ref[...]
    softplus = jnp.maximum(z, 0.0) + jnp.log1p(jnp.exp(-jnp.abs(z)))
    half_rate = ((-0.5 * LRU_C * LOG2E) * softplus).reshape(1, 1, RG_W)
    a = jnp.exp2(tr * half_rate + half_rate)
    d = jnp.maximum(1.0 - a * a, 0.0)
    mult = jnp.where(d > 0.0, d * lax.rsqrt(d), 0.0)
    a_ref[...] = a
    u_ref[...] = mult * (xh * (1.0 + ti))

    def body(t, h):
        h = a_ref[t] * h + u_ref[t]
        u_ref[t] = h
        return h

    h_ref[...] = lax.fori_loop(0, ts, body, h_ref[...], unroll=8)

    for sb in range(ts // sub):
        t0 = sb * sub
        y = (u_ref[t0:t0 + sub] * _silu(g_scr[t0:t0 + sub])).reshape(rows, RG_W).astype(BF16)
        bm = jnp.dot(to_batch_major, y, preferred_element_type=F32)
        y_ref[:, t0:t0 + sub, :] = bm.reshape(nb, sub, RG_W).astype(BF16)


def _block_diag(w, group):
    h, di, dj = w.shape
    w = w.reshape(h // group, group, di, dj)
    eye = jnp.eye(group, dtype=w.dtype)
    return (eye[None, :, None, :, None] * w[:, :, :, None, :]).reshape(h // group, group * di, group * dj)


def _rglru(proj, conv_w, conv_b, w_rg_a, b_rg_a, w_rg_x, b_rg_x, lru_lambda, ts):
    nb, s, _ = proj.shape
    group = MXU_DIM // RG_HEAD_DIM
    n_blk = RG_HEADS // group
    wa = _block_diag(w_rg_a, group).astype(BF16)
    wx = _block_diag(w_rg_x, group).astype(BF16)
    row = lambda v: v.reshape(1, RG_W)
    const2 = lambda shape: pl.BlockSpec(shape, lambda i: (0, 0))
    const3 = lambda shape: pl.BlockSpec(shape, lambda i: (0, 0, 0))
    return pl.pallas_call(
        functools.partial(_rglru_kernel, ts=ts, nb=nb),
        grid=(s // ts,),
        in_specs=[pl.BlockSpec((nb, ts, 2 * RG_W), lambda i: (0, i, 0)),
                  const2((CONV_WIDTH, RG_W)), const2((1, RG_W)),
                  const3((n_blk, MXU_DIM, MXU_DIM)), const2((1, RG_W)),
                  const3((n_blk, MXU_DIM, MXU_DIM)), const2((1, RG_W)),
                  const2((1, RG_W))],
        out_specs=pl.BlockSpec((nb, ts, RG_W), lambda i: (0, i, 0)),
        out_shape=jax.ShapeDtypeStruct((nb, s, RG_W), BF16),
        scratch_shapes=[
            pltpu.VMEM((ts + CONV_WIDTH - 1, nb, RG_W), F32),
            pltpu.VMEM((ts, nb, RG_W), F32),
            pltpu.VMEM((ts, nb, RG_W), F32),
            pltpu.VMEM((ts, nb, RG_W), F32),
            pltpu.VMEM((nb, RG_W), F32),
        ],
        compiler_params=_params(("arbitrary",)),
        name="rglru",
    )(proj, conv_w, row(conv_b), wa, row(b_rg_a), wx, row(b_rg_x), row(lru_lambda))


def _attn_kernel(q_ref, k_ref, v_ref, g_ref, xq_ref, xg_ref, mem_ref, gm_ref, wm_ref,
                 lq1_ref, lk1_ref, lq2_ref, lk2_ref, gs_ref,
                 yda_ref, yxa_ref, vt_ref, km_ref, vmt_ref, *, tq):
    h = pl.program_id(1)
    hd = DA_V_DIM
    dh = XA_HEAD_DIM
    s_len = q_ref.shape[0]
    n_mem = mem_ref.shape[0]
    nq = s_len // tq
    w = 2 * tq

    @pl.when(h == 0)
    def _():
        mn = _rms(mem_ref[...], gm_ref[...]).astype(BF16)
        kv = jnp.dot(mn, wm_ref[...].astype(BF16), preferred_element_type=F32)
        for hh in range(XA_HEADS):
            km_ref[hh] = kv[:, hh * dh:(hh + 1) * dh].astype(BF16)
            lo = XA_W + hh * dh
            vmt_ref[hh, 0:dh, :] = kv[:, lo:lo + dh].T.astype(BF16)
            vmt_ref[hh, dh:dh + BF16_ROWS, :] = jnp.ones((BF16_ROWS, n_mem), BF16)

    vt_ref[0:hd, :] = v_ref[...].astype(F32).T.astype(BF16)
    vt_ref[hd:hd + BF16_ROWS, :] = jnp.ones((BF16_ROWS, s_len), BF16)

    lam = (jnp.exp(jnp.sum(lq1_ref[...] * lk1_ref[...], axis=-1, keepdims=True))
           - jnp.exp(jnp.sum(lq2_ref[...] * lk2_ref[...], axis=-1, keepdims=True))
           + LAMBDA_INIT)
    lane = lax.broadcasted_iota(jnp.int32, (tq, hd), 1)
    key = lax.broadcasted_iota(jnp.int32, (tq, w), 0)
    qry = lax.broadcasted_iota(jnp.int32, (tq, w), 1) & (tq - 1)
    causal = key <= qry

    def softmax_t(s3, vt):
        n = s3.shape[0] * SUBLANES
        m = jnp.max(jnp.max(s3, axis=0), axis=0, keepdims=True)
        p = jnp.exp2(s3 - m[None]).reshape(n, s3.shape[2]).astype(BF16)
        return jnp.dot(vt, p, preferred_element_type=F32)

    def da_scores(qb):
        kv = (qb + 1) * tq
        q = q_ref[qb * tq:kv, :].astype(F32)
        qq = jnp.concatenate([jnp.where(lane < DA_HEAD_DIM, q, 0.0),
                              jnp.where(lane >= DA_HEAD_DIM, q, 0.0)], axis=0)
        qqt = qq.T.astype(BF16)
        s_diag = jnp.dot(k_ref[kv - tq:kv, :], qqt, preferred_element_type=F32)
        parts = [jnp.where(causal, s_diag, -jnp.inf)]
        if qb > 0:
            parts.insert(0, jnp.dot(k_ref[0:kv - tq, :], qqt, preferred_element_type=F32))
        return jnp.concatenate(parts, axis=0).reshape(kv // SUBLANES, SUBLANES, w)

    def da_finish(qb, s3):
        rows = slice(qb * tq, (qb + 1) * tq)
        pv = softmax_t(s3, vt_ref[:, 0:(qb + 1) * tq])
        ot = pv[0:hd] / pv[hd:hd + 1]
        od = (ot[:, 0:tq] - lam * ot[:, tq:w]).T
        o = _rms(od, gs_ref[...]) * (1.0 - LAMBDA_INIT)
        yda_ref[rows, :] = (o * _silu(g_ref[rows, :].astype(F32))).astype(BF16)

    def xa_scores(_):
        return _nt_dot(km_ref[h], xq_ref[...]).reshape(n_mem // SUBLANES, SUBLANES, s_len)

    def xa_finish(_, s3):
        pv = softmax_t(s3, vmt_ref[h])
        o = (pv[0:dh] / pv[dh:dh + 1]).T
        yxa_ref[...] = (o * _silu(xg_ref[...].astype(F32))).astype(BF16)

    chains = [(da_scores, da_finish, 0), (xa_scores, xa_finish, None)]
    chains += [(da_scores, da_finish, qb) for qb in range(nq - 1, 0, -1)]
    ahead = 2
    pending = [sc(arg) for sc, _, arg in chains[:ahead]]
    for pos, (_, finish, arg) in enumerate(chains):
        s3 = pending.pop(0)
        if pos + ahead < len(chains):
            sc, _, nxt = chains[pos + ahead]
            pending.append(sc(nxt))
        finish(arg, s3)


def _attn(proj, mem, g_mem, w_mem_kv, lq1, lk1, lq2, lk2, g_subln, tq):
    b, s, _ = proj.shape
    m, d = mem.shape[1:]
    hd = DA_V_DIM
    assert XA_HEAD_DIM == hd and XA_HEADS == DA_HEADS
    per_chunk = PROJ_CHUNK // hd
    vec = lambda v: v.reshape(1, -1)
    cvec = lambda n: pl.BlockSpec((1, n), lambda bi, h: (0, 0))
    col = lambda chunk: pl.BlockSpec((None, s, hd), lambda bi, h: (bi, 0, chunk * per_chunk + h))
    out = pl.BlockSpec((None, s, hd), lambda bi, h: (bi, 0, h))
    return pl.pallas_call(
        functools.partial(_attn_kernel, tq=tq),
        grid=(b, DA_HEADS),
        in_specs=[col(C_DA_Q), col(C_DA_K), col(C_DA_V), col(C_DA_G), col(C_XA_Q), col(C_XA_G),
                  pl.BlockSpec((None, m, d), lambda bi, h: (bi, 0, 0)),
                  cvec(d),
                  _resident((d, 2 * XA_W), lambda bi, h: (0, 0)),
                  cvec(DA_HEAD_DIM), cvec(DA_HEAD_DIM), cvec(DA_HEAD_DIM), cvec(DA_HEAD_DIM),
                  cvec(DA_V_DIM)],
        out_specs=[out, out],
        out_shape=[jax.ShapeDtypeStruct((b, s, DA_W), BF16), jax.ShapeDtypeStruct((b, s, XA_W), BF16)],
        scratch_shapes=[pltpu.VMEM((hd + BF16_ROWS, s), BF16),
                        pltpu.VMEM((XA_HEADS, m, hd), BF16),
                        pltpu.VMEM((XA_HEADS, hd + BF16_ROWS, m), BF16)],
        compiler_params=_params(("arbitrary", "arbitrary")),
        name="attn",
    )(proj, proj, proj, proj, proj, proj, mem, vec(g_mem), w_mem_kv,
      vec(lq1), vec(lk1), vec(lq2), vec(lk2), vec(g_subln))


def _outproj_kernel(yrg_ref, yda_ref, yxa_ref, w_ref, x_ref, g_ref, o_ref):
    y = jnp.dot(yrg_ref[...], w_ref[0:RG_W, :].astype(BF16), preferred_element_type=F32)
    y = y + jnp.dot(yda_ref[...], w_ref[RG_W:RG_W + DA_W, :].astype(BF16), preferred_element_type=F32)
    y = y + jnp.dot(yxa_ref[...], w_ref[RG_W + DA_W:, :].astype(BF16), preferred_element_type=F32)
    o_ref[...] = x_ref[...] + _rms(y, g_ref[...])


def _outproj(y_rg, y_da, y_xa, w_out, x, g_post, tm):
    b, s, d = x.shape
    d_mix = w_out.shape[0]
    tile = lambda width: pl.BlockSpec((None, tm, width), lambda i, j: (i, j, 0))
    return pl.pallas_call(
        _outproj_kernel,
        grid=(b, s // tm),
        in_specs=[
            tile(RG_W), tile(DA_W), tile(XA_W),
            _resident((d_mix, d), lambda i, j: (0, 0)),
            tile(d),
            pl.BlockSpec((1, d), lambda i, j: (0, 0)),
        ],
        out_specs=tile(d),
        out_shape=jax.ShapeDtypeStruct((b, s, d), F32),
        compiler_params=_params(("arbitrary", "arbitrary")),
        name="outproj",
    )(y_rg, y_da, y_xa, w_out, x, g_post.reshape(1, d))


def kernel(x, mem, g_pre, g_mem, w_in, w_mem_kv, conv_w, conv_b, w_rg_a, b_rg_a, w_rg_x, b_rg_x,
           lru_lambda, lambda_q1, lambda_k1, lambda_q2, lambda_k2, g_subln, w_out, g_post):
    proj = _inproj(x, g_pre, w_in, tm=1024)
    y_rg = _rglru(proj, conv_w, conv_b, w_rg_a, b_rg_a, w_rg_x, b_rg_x, lru_lambda, ts=128)
    y_da, y_xa = _attn(proj, mem, g_mem, w_mem_kv,
                       lambda_q1, lambda_k1, lambda_q2, lambda_k2, g_subln, tq=256)
    return _outproj(y_rg, y_da, y_xa, w_out, x, g_post, tm=1024)
```

```python
import functools
import math

import jax
import jax.numpy as jnp
from jax import lax
from jax.experimental import pallas as pl
from jax.experimental.pallas import tpu as pltpu

F32 = jnp.float32
BF16 = jnp.bfloat16

RMS_EPS = 1e-6
N_MEM = 256
RG_HEADS = 8
RG_HEAD_DIM = 64
RG_W = RG_HEADS * RG_HEAD_DIM
CONV_WIDTH = 4
LRU_C = 8.0
DA_HEADS = 4
DA_HEAD_DIM = 64
DA_V_DIM = 2 * DA_HEAD_DIM
DA_W = DA_HEADS * DA_V_DIM
LAMBDA_INIT = 0.8 - 0.6 * math.exp(-0.3 * (1 - 1))
XA_HEADS = 4
XA_HEAD_DIM = 128
XA_W = XA_HEADS * XA_HEAD_DIM
LOG2E = 1.4426950408889634

PROJ_CHUNK = 512
(C_RG_X, C_RG_G, C_DA_Q, C_DA_K, C_DA_V, C_DA_G, C_XA_Q, C_XA_G) = range(8)
D_IN = 8 * PROJ_CHUNK

SUBLANES = 8
LANES = 128
BF16_ROWS = 16
MXU_DIM = 256
VMEM_LIMIT_BYTES = 56 * 1024 * 1024


def _resident(shape, index_map):
    return pl.BlockSpec(shape, index_map, pipeline_mode=pl.Buffered(1))


def _params(sem):
    return pltpu.CompilerParams(dimension_semantics=sem, vmem_limit_bytes=VMEM_LIMIT_BYTES)


def _rms(x, g):
    ms = jnp.mean(x * x, axis=-1, keepdims=True)
    return x * lax.rsqrt(ms + RMS_EPS) * g


def _silu(x):
    h = 0.5 * x
    return h * (1.0 + jnp.tanh(h))


def _nt_dot(a, b):
    return lax.dot_general(a, b, (((1,), (1,)), ((), ())), preferred_element_type=F32)


RG_SUB = MXU_DIM // SUBLANES
N_RG_CHUNKS = 2


def _block_diag(w, group):
    h, di, dj = w.shape
    w = w.reshape(h // group, group, di, dj)
    eye = jnp.eye(group, dtype=w.dtype)
    return (eye[None, :, None, :, None] * w[:, :, :, None, :]).reshape(h // group, group * di, group * dj)


def _inproj_rglru_kernel(x_ref, g_ref, w_ref, cw_ref, cb_ref, wa_ref, ba_ref, wx_ref, bx_ref, lam_ref,
                         proj_ref, y_ref, rg_scr, xp_ref, g_scr, a_ref, u_ref, h_ref, *, nt, nb, ts):
    j = pl.program_id(0)
    b = pl.program_id(1)
    halo = CONV_WIDTH - 1
    sub = RG_SUB
    rows = sub * nb
    tm = x_ref.shape[0]
    n_sub = ts // sub
    assert rows == MXU_DIM and nb == SUBLANES and ts % sub == 0 and tm == nb * ts
    q_scale = {C_DA_Q: (DA_HEAD_DIM ** -0.5) * LOG2E, C_XA_Q: (XA_HEAD_DIM ** -0.5) * LOG2E}

    def normed():
        return _rms(x_ref[...], g_ref[...]).astype(BF16)

    def project(hn, chunks):
        for n in chunks:
            cols = slice(n * PROJ_CHUNK, (n + 1) * PROJ_CHUNK)
            acc = jnp.dot(hn, w_ref[:, cols].astype(BF16), preferred_element_type=F32)
            if n in q_scale:
                acc = acc * q_scale[n]
            if n < N_RG_CHUNKS:
                rg_scr[j & 1, b, :, :, cols] = acc.astype(BF16).reshape(tm // sub, sub, PROJ_CHUNK)
            else:
                out_cols = slice((n - N_RG_CHUNKS) * PROJ_CHUNK, (n - N_RG_CHUNKS + 1) * PROJ_CHUNK)
                proj_ref[:, out_cols] = acc.astype(BF16)

    def permutations():
        r = lax.broadcasted_iota(jnp.int32, (rows, rows), 0)
        c = lax.broadcasted_iota(jnp.int32, (rows, rows), 1)
        log_nb, log_sub = nb.bit_length() - 1, sub.bit_length() - 1
        to_time_major = (c == (r & (nb - 1)) * sub + (r >> log_nb)).astype(BF16)
        to_batch_major = (c == (r & (sub - 1)) * nb + (r >> log_sub)).astype(BF16)
        return to_time_major, to_batch_major

    def rg_load(to_time_major):
        xp_ref[0:halo] = xp_ref[ts:ts + halo]
        prev = (j + 1) & 1
        for sb in range(n_sub):
            t0 = sb * sub
            blk = rg_scr[prev, :, b * n_sub + sb].reshape(rows, 2 * RG_W)
            tmaj = jnp.dot(to_time_major, blk, preferred_element_type=F32).reshape(sub, nb, 2 * RG_W)
            xp_ref[halo + t0:halo + t0 + sub] = tmaj[:, :, 0:RG_W]
            g_scr[t0:t0 + sub] = tmaj[:, :, RG_W:2 * RG_W]

    def rg_recur():
        xh = 0.5 * cb_ref[...].reshape(1, 1, RG_W)
        for k in range(CONV_WIDTH):
            xh = xh + (0.5 * cw_ref[k:k + 1, :]).reshape(1, 1, RG_W) * xp_ref[k:k + ts]
        xb = xh.reshape(ts * nb, RG_W).astype(BF16)

        def gate(wg_ref, bg_ref):
            z = [jnp.dot(xb[:, m * MXU_DIM:(m + 1) * MXU_DIM], wg_ref[m], preferred_element_type=F32)
                 for m in range(RG_W // MXU_DIM)]
            return jnp.tanh(jnp.concatenate(z, axis=1) + 0.5 * bg_ref[...]).reshape(ts, nb, RG_W)

        tr = gate(wa_ref, ba_ref)
        ti = gate(wx_ref, bx_ref)
        z = -lam_ref[...]
        softplus = jnp.maximum(z, 0.0) + jnp.log1p(jnp.exp(-jnp.abs(z)))
        half_rate = ((-0.5 * LRU_C * LOG2E) * softplus).reshape(1, 1, RG_W)
        a = jnp.exp2(tr * half_rate + half_rate)
        d = jnp.maximum(1.0 - a * a, 0.0)
        mult = jnp.where(d > 0.0, d * lax.rsqrt(d), 0.0)
        a_ref[...] = a
        u_ref[...] = mult * (xh * (1.0 + ti))
        h = h_ref[...]
        for t in range(ts):
            h = a_ref[t] * h + u_ref[t]
            u_ref[t] = h
        h_ref[...] = h

    def rg_store(to_batch_major):
        for sb in range(n_sub):
            t0 = sb * sub
            y = (u_ref[t0:t0 + sub] * _silu(g_scr[t0:t0 + sub])).reshape(rows, RG_W).astype(BF16)
            bm = jnp.dot(to_batch_major, y, preferred_element_type=F32)
            y_ref[:, t0:t0 + sub, :] = bm.reshape(nb, sub, RG_W).astype(BF16)

    n_chunks = D_IN // PROJ_CHUNK

    @pl.when(j == 0)
    def _():
        xp_ref[...] = jnp.zeros(xp_ref.shape, F32)
        h_ref[...] = jnp.zeros(h_ref.shape, F32)
        project(normed(), range(n_chunks))

    @pl.when(jnp.logical_and(j > 0, j < nt))
    def _():
        to_tm, to_bm = permutations()
        hn = normed()
        rg_load(to_tm)
        project(hn, range(0, n_chunks // 2))
        rg_recur()
        project(hn, range(n_chunks // 2, n_chunks))
        rg_store(to_bm)

    @pl.when(j == nt)
    def _():
        to_tm, to_bm = permutations()
        rg_load(to_tm)
        rg_recur()
        rg_store(to_bm)


def _inproj_rglru(x, g_pre, w_in, conv_w, conv_b, w_rg_a, b_rg_a, w_rg_x, b_rg_x, lru_lambda, ts):
    nb, s, d = x.shape
    tm = nb * ts
    nt = s // tm
    group = MXU_DIM // RG_HEAD_DIM
    n_blk = RG_HEADS // group
    wa = _block_diag(w_rg_a, group).astype(BF16)
    wx = _block_diag(w_rg_x, group).astype(BF16)
    row = lambda v: v.reshape(1, -1)
    const2 = lambda shape: pl.BlockSpec(shape, lambda j, b: (0, 0))
    const3 = lambda shape: pl.BlockSpec(shape, lambda j, b: (0, 0, 0))
    proj_idx = lambda j, b: (jnp.where(j < nt, b, nb - 1), jnp.minimum(j, nt - 1), 0)
    y_idx = lambda j, b: (0, jnp.where(j > 0, (j - 1) * nb + b, 0), 0)
    w_out = D_IN - N_RG_CHUNKS * PROJ_CHUNK
    return pl.pallas_call(
        functools.partial(_inproj_rglru_kernel, nt=nt, nb=nb, ts=ts),
        grid=(nt + 1, nb),
        in_specs=[pl.BlockSpec((None, tm, d), proj_idx),
                  const2((1, d)),
                  _resident((d, D_IN), lambda j, b: (0, 0)),
                  const2((CONV_WIDTH, RG_W)), const2((1, RG_W)),
                  const3((n_blk, MXU_DIM, MXU_DIM)), const2((1, RG_W)),
                  const3((n_blk, MXU_DIM, MXU_DIM)), const2((1, RG_W)),
                  const2((1, RG_W))],
        out_specs=[pl.BlockSpec((None, tm, w_out), proj_idx),
                   pl.BlockSpec((nb, ts, RG_W), y_idx)],
        out_shape=[jax.ShapeDtypeStruct((nb, s, w_out), BF16),
                   jax.ShapeDtypeStruct((nb, s, RG_W), BF16)],
        scratch_shapes=[
            pltpu.VMEM((2, nb, tm // RG_SUB, RG_SUB, N_RG_CHUNKS * PROJ_CHUNK), BF16),
            pltpu.VMEM((ts + CONV_WIDTH - 1, nb, RG_W), F32),
            pltpu.VMEM((ts, nb, RG_W), F32),
            pltpu.VMEM((ts, nb, RG_W), F32),
            pltpu.VMEM((ts, nb, RG_W), F32),
            pltpu.VMEM((nb, RG_W), F32),
        ],
        compiler_params=_params(("arbitrary", "arbitrary")),
        name="inproj_rglru",
    )(x, row(g_pre), w_in, conv_w, row(conv_b), wa, row(b_rg_a), wx, row(b_rg_x), row(lru_lambda))


def _attn_kernel(q_ref, k_ref, v_ref, g_ref, xq_ref, xg_ref, mem_ref, gm_ref, wm_ref,
                 lq1_ref, lk1_ref, lq2_ref, lk2_ref, gs_ref,
                 yda_ref, yxa_ref, vt_ref, km_ref, vmt_ref, *, tq):
    h = pl.program_id(1)
    hd = DA_V_DIM
    dh = XA_HEAD_DIM
    s_len = q_ref.shape[0]
    n_mem = mem_ref.shape[0]
    nq = s_len // tq
    w = 2 * tq

    @pl.when(h == 0)
    def _():
        mn = _rms(mem_ref[...], gm_ref[...]).astype(BF16)
        kv = jnp.dot(mn, wm_ref[...].astype(BF16), preferred_element_type=F32)
        for hh in range(XA_HEADS):
            km_ref[hh] = kv[:, hh * dh:(hh + 1) * dh].astype(BF16)
            lo = XA_W + hh * dh
            vmt_ref[hh, 0:dh, :] = kv[:, lo:lo + dh].T.astype(BF16)
            vmt_ref[hh, dh:dh + BF16_ROWS, :] = jnp.ones((BF16_ROWS, n_mem), BF16)

    vt_ref[0:hd, :] = v_ref[...].astype(F32).T.astype(BF16)
    vt_ref[hd:hd + BF16_ROWS, :] = jnp.ones((BF16_ROWS, s_len), BF16)

    lam = (jnp.exp(jnp.sum(lq1_ref[...] * lk1_ref[...], axis=-1, keepdims=True))
           - jnp.exp(jnp.sum(lq2_ref[...] * lk2_ref[...], axis=-1, keepdims=True))
           + LAMBDA_INIT)
    lane = lax.broadcasted_iota(jnp.int32, (tq, hd), 1)
    key = lax.broadcasted_iota(jnp.int32, (tq, w), 0)
    qry = lax.broadcasted_iota(jnp.int32, (tq, w), 1) & (tq - 1)
    causal = key <= qry

    def softmax_t(s3, vt):
        n = s3.shape[0] * SUBLANES
        m = jnp.max(jnp.max(s3, axis=0), axis=0, keepdims=True)
        p = jnp.exp2(s3 - m[None]).reshape(n, s3.shape[2]).astype(BF16)
        return jnp.dot(vt, p, preferred_element_type=F32)

    def da_scores(qb):
        kv = (qb + 1) * tq
        q = q_ref[qb * tq:kv, :].astype(F32)
        qq = jnp.concatenate([jnp.where(lane < DA_HEAD_DIM, q, 0.0),
                              jnp.where(lane >= DA_HEAD_DIM, q, 0.0)], axis=0)
        qqt = qq.T.astype(BF16)
        s_diag = jnp.dot(k_ref[kv - tq:kv, :], qqt, preferred_element_type=F32)
        parts = [jnp.where(causal, s_diag, -jnp.inf)]
        if qb > 0:
            parts.insert(0, jnp.dot(k_ref[0:kv - tq, :], qqt, preferred_element_type=F32))
        return jnp.concatenate(parts, axis=0).reshape(kv // SUBLANES, SUBLANES, w)

    def da_finish(qb, s3):
        rows = slice(qb * tq, (qb + 1) * tq)
        pv = softmax_t(s3, vt_ref[:, 0:(qb + 1) * tq])
        ot = pv[0:hd] / pv[hd:hd + 1]
        od = (ot[:, 0:tq] - lam * ot[:, tq:w]).T
        o = _rms(od, gs_ref[...]) * (1.0 - LAMBDA_INIT)
        yda_ref[rows, :] = (o * _silu(g_ref[rows, :].astype(F32))).astype(BF16)

    def xa_scores(_):
        return _nt_dot(km_ref[h], xq_ref[...]).reshape(n_mem // SUBLANES, SUBLANES, s_len)

    def xa_finish(_, s3):
        pv = softmax_t(s3, vmt_ref[h])
        o = (pv[0:dh] / pv[dh:dh + 1]).T
        yxa_ref[...] = (o * _silu(xg_ref[...].astype(F32))).astype(BF16)

    chains = [(da_scores, da_finish, 0), (xa_scores, xa_finish, None)]
    chains += [(da_scores, da_finish, qb) for qb in range(nq - 1, 0, -1)]
    ahead = 2
    pending = [sc(arg) for sc, _, arg in chains[:ahead]]
    for pos, (_, finish, arg) in enumerate(chains):
        s3 = pending.pop(0)
        if pos + ahead < len(chains):
            sc, _, nxt = chains[pos + ahead]
            pending.append(sc(nxt))
        finish(arg, s3)


def _attn(proj, mem, g_mem, w_mem_kv, lq1, lk1, lq2, lk2, g_subln, tq):
    b, s, _ = proj.shape
    m, d = mem.shape[1:]
    hd = DA_V_DIM
    assert XA_HEAD_DIM == hd and XA_HEADS == DA_HEADS
    per_chunk = PROJ_CHUNK // hd
    vec = lambda v: v.reshape(1, -1)
    cvec = lambda n: pl.BlockSpec((1, n), lambda bi, h: (0, 0))
    col = lambda chunk: pl.BlockSpec((None, s, hd), lambda bi, h: (bi, 0, (chunk - N_RG_CHUNKS) * per_chunk + h))
    out = pl.BlockSpec((None, s, hd), lambda bi, h: (bi, 0, h))
    return pl.pallas_call(
        functools.partial(_attn_kernel, tq=tq),
        grid=(b, DA_HEADS),
        in_specs=[col(C_DA_Q), col(C_DA_K), col(C_DA_V), col(C_DA_G), col(C_XA_Q), col(C_XA_G),
                  pl.BlockSpec((None, m, d), lambda bi, h: (bi, 0, 0)),
                  cvec(d),
                  _resident((d, 2 * XA_W), lambda bi, h: (0, 0)),
                  cvec(DA_HEAD_DIM), cvec(DA_HEAD_DIM), cvec(DA_HEAD_DIM), cvec(DA_HEAD_DIM),
                  cvec(DA_V_DIM)],
        out_specs=[out, out],
        out_shape=[jax.ShapeDtypeStruct((b, s, DA_W), BF16), jax.ShapeDtypeStruct((b, s, XA_W), BF16)],
        scratch_shapes=[pltpu.VMEM((hd + BF16_ROWS, s), BF16),
                        pltpu.VMEM((XA_HEADS, m, hd), BF16),
                        pltpu.VMEM((XA_HEADS, hd + BF16_ROWS, m), BF16)],
        compiler_params=_params(("arbitrary", "arbitrary")),
        name="attn",
    )(proj, proj, proj, proj, proj, proj, mem, vec(g_mem), w_mem_kv,
      vec(lq1), vec(lk1), vec(lq2), vec(lk2), vec(g_subln))


def _outproj_kernel(yrg_ref, yda_ref, yxa_ref, w_ref, x_ref, g_ref, o_ref):
    y = jnp.dot(yrg_ref[...], w_ref[0:RG_W, :].astype(BF16), preferred_element_type=F32)
    y = y + jnp.dot(yda_ref[...], w_ref[RG_W:RG_W + DA_W, :].astype(BF16), preferred_element_type=F32)
    y = y + jnp.dot(yxa_ref[...], w_ref[RG_W + DA_W:, :].astype(BF16), preferred_element_type=F32)
    o_ref[...] = x_ref[...] + _rms(y, g_ref[...])


def _outproj(y_rg, y_da, y_xa, w_out, x, g_post, tm):
    b, s, d = x.shape
    d_mix = w_out.shape[0]
    tile = lambda width: pl.BlockSpec((None, tm, width), lambda i, j: (i, j, 0))
    return pl.pallas_call(
        _outproj_kernel,
        grid=(b, s // tm),
        in_specs=[
            tile(RG_W), tile(DA_W), tile(XA_W),
            _resident((d_mix, d), lambda i, j: (0, 0)),
            tile(d),
            pl.BlockSpec((1, d), lambda i, j: (0, 0)),
        ],
        out_specs=tile(d),
        out_shape=jax.ShapeDtypeStruct((b, s, d), F32),
        compiler_params=_params(("arbitrary", "arbitrary")),
        name="outproj",
    )(y_rg, y_da, y_xa, w_out, x, g_post.reshape(1, d))


def kernel(x, mem, g_pre, g_mem, w_in, w_mem_kv, conv_w, conv_b, w_rg_a, b_rg_a, w_rg_x, b_rg_x,
           lru_lambda, lambda_q1, lambda_k1, lambda_q2, lambda_k2, g_subln, w_out, g_post):
    proj, y_rg = _inproj_rglru(x, g_pre, w_in, conv_w, conv_b, w_rg_a, b_rg_a, w_rg_x, b_rg_x,
                               lru_lambda, ts=64)
    y_da, y_xa = _attn(proj, mem, g_mem, w_mem_kv,
                       lambda_q1, lambda_k1, lambda_q2, lambda_k2, g_subln, tq=256)
    return _outproj(y_rg, y_da, y_xa, w_out, x, g_post, tm=1024)
```

```python
import functools
import math

import jax
import jax.numpy as jnp
from jax import lax
from jax.experimental import pallas as pl
from jax.experimental.pallas import tpu as pltpu

F32 = jnp.float32
BF16 = jnp.bfloat16

RMS_EPS = 1e-6
N_MEM = 256
RG_HEADS = 8
RG_HEAD_DIM = 64
RG_W = RG_HEADS * RG_HEAD_DIM
CONV_WIDTH = 4
LRU_C = 8.0
DA_HEADS = 4
DA_HEAD_DIM = 64
DA_V_DIM = 2 * DA_HEAD_DIM
DA_W = DA_HEADS * DA_V_DIM
LAMBDA_INIT = 0.8 - 0.6 * math.exp(-0.3 * (1 - 1))
XA_HEADS = 4
XA_HEAD_DIM = 128
XA_W = XA_HEADS * XA_HEAD_DIM
LOG2E = 1.4426950408889634

PROJ_CHUNK = 512
(C_RG_X, C_RG_G, C_DA_Q, C_DA_K, C_DA_V, C_DA_G, C_XA_Q, C_XA_G) = range(8)
D_IN = 8 * PROJ_CHUNK

SUBLANES = 8
LANES = 128
BF16_ROWS = 16
MXU_DIM = 256
VMEM_LIMIT_BYTES = 56 * 1024 * 1024


def _resident(shape, index_map):
    return pl.BlockSpec(shape, index_map, pipeline_mode=pl.Buffered(1))


def _params(sem):
    return pltpu.CompilerParams(dimension_semantics=sem, vmem_limit_bytes=VMEM_LIMIT_BYTES)


def _rms(x, g):
    ms = jnp.mean(x * x, axis=-1, keepdims=True)
    return x * lax.rsqrt(ms + RMS_EPS) * g


def _silu(x):
    h = 0.5 * x
    return h * (1.0 + jnp.tanh(h))


def _nt_dot(a, b):
    return lax.dot_general(a, b, (((1,), (1,)), ((), ())), preferred_element_type=F32)


RG_SUB = MXU_DIM // SUBLANES
N_RG_CHUNKS = 2


def _block_diag(w, group):
    h, di, dj = w.shape
    w = w.reshape(h // group, group, di, dj)
    eye = jnp.eye(group, dtype=w.dtype)
    return (eye[None, :, None, :, None] * w[:, :, :, None, :]).reshape(h // group, group * di, group * dj)


def _inproj_rglru_kernel(x_ref, g_ref, w_ref, cw_ref, cb_ref, wa_ref, ba_ref, wx_ref, bx_ref, lam_ref,
                         proj_ref, y_ref, rg_scr, xp_ref, g_scr, a_ref, u_ref, h_ref, *, nt, nb, ts):
    j = pl.program_id(0)
    b = pl.program_id(1)
    halo = CONV_WIDTH - 1
    sub = RG_SUB
    rows = sub * nb
    tm = x_ref.shape[0]
    n_sub = ts // sub
    assert rows == MXU_DIM and nb == SUBLANES and ts % sub == 0 and tm == nb * ts
    q_scale = {C_DA_Q: (DA_HEAD_DIM ** -0.5) * LOG2E, C_XA_Q: (XA_HEAD_DIM ** -0.5) * LOG2E}

    def normed():
        return _rms(x_ref[...], g_ref[...]).astype(BF16)

    def project(hn, chunks):
        for n in chunks:
            cols = slice(n * PROJ_CHUNK, (n + 1) * PROJ_CHUNK)
            acc = jnp.dot(hn, w_ref[:, cols].astype(BF16), preferred_element_type=F32)
            if n in q_scale:
                acc = acc * q_scale[n]
            if n < N_RG_CHUNKS:
                rg_scr[j & 1, b, :, :, cols] = acc.astype(BF16).reshape(tm // sub, sub, PROJ_CHUNK)
            else:
                out_cols = slice((n - N_RG_CHUNKS) * PROJ_CHUNK, (n - N_RG_CHUNKS + 1) * PROJ_CHUNK)
                proj_ref[:, out_cols] = acc.astype(BF16)

    def permutations():
        r = lax.broadcasted_iota(jnp.int32, (rows, rows), 0)
        c = lax.broadcasted_iota(jnp.int32, (rows, rows), 1)
        log_nb, log_sub = nb.bit_length() - 1, sub.bit_length() - 1
        to_time_major = (c == (r & (nb - 1)) * sub + (r >> log_nb)).astype(BF16)
        to_batch_major = (c == (r & (sub - 1)) * nb + (r >> log_sub)).astype(BF16)
        return to_time_major, to_batch_major

    def rg_load(to_time_major):
        xp_ref[0:halo] = xp_ref[ts:ts + halo]
        prev = (j + 1) & 1
        for sb in range(n_sub):
            t0 = sb * sub
            blk = rg_scr[prev, :, b * n_sub + sb].reshape(rows, 2 * RG_W)
            tmaj = jnp.dot(to_time_major, blk, preferred_element_type=F32).reshape(sub, nb, 2 * RG_W)
            xp_ref[halo + t0:halo + t0 + sub] = tmaj[:, :, 0:RG_W]
            g_scr[t0:t0 + sub] = tmaj[:, :, RG_W:2 * RG_W]

    def rg_recur():
        xh = 0.5 * cb_ref[...].reshape(1, 1, RG_W)
        for k in range(CONV_WIDTH):
            xh = xh + (0.5 * cw_ref[k:k + 1, :]).reshape(1, 1, RG_W) * xp_ref[k:k + ts]
        xb = xh.reshape(ts * nb, RG_W).astype(BF16)

        def gate(wg_ref, bg_ref):
            z = [jnp.dot(xb[:, m * MXU_DIM:(m + 1) * MXU_DIM], wg_ref[m], preferred_element_type=F32)
                 for m in range(RG_W // MXU_DIM)]
            return jnp.tanh(jnp.concatenate(z, axis=1) + 0.5 * bg_ref[...]).reshape(ts, nb, RG_W)

        tr = gate(wa_ref, ba_ref)
        ti = gate(wx_ref, bx_ref)
        z = -lam_ref[...]
        softplus = jnp.maximum(z, 0.0) + jnp.log1p(jnp.exp(-jnp.abs(z)))
        half_rate = ((-0.5 * LRU_C * LOG2E) * softplus).reshape(1, 1, RG_W)
        a = jnp.exp2(tr * half_rate + half_rate)
        d = jnp.maximum(1.0 - a * a, 0.0)
        mult = jnp.where(d > 0.0, d * lax.rsqrt(d), 0.0)
        a_ref[...] = a
        u_ref[...] = mult * (xh * (1.0 + ti))
        h = h_ref[...]
        for t in range(ts):
            h = a_ref[t] * h + u_ref[t]
            u_ref[t] = h
        h_ref[...] = h

    def rg_store(to_batch_major):
        for sb in range(n_sub):
            t0 = sb * sub
            y = (u_ref[t0:t0 + sub] * _silu(g_scr[t0:t0 + sub])).reshape(rows, RG_W).astype(BF16)
            bm = jnp.dot(to_batch_major, y, preferred_element_type=F32)
            y_ref[:, t0:t0 + sub, :] = bm.reshape(nb, sub, RG_W).astype(BF16)

    n_chunks = D_IN // PROJ_CHUNK

    @pl.when(j == 0)
    def _():
        xp_ref[...] = jnp.zeros(xp_ref.shape, F32)
        h_ref[...] = jnp.zeros(h_ref.shape, F32)
        project(normed(), range(n_chunks))

    @pl.when(jnp.logical_and(j > 0, j < nt))
    def _():
        to_tm, to_bm = permutations()
        hn = normed()
        rg_load(to_tm)
        project(hn, range(0, n_chunks // 2))
        rg_recur()
        project(hn, range(n_chunks // 2, n_chunks))
        rg_store(to_bm)

    @pl.when(j == nt)
    def _():
        to_tm, to_bm = permutations()
        rg_load(to_tm)
        rg_recur()
        rg_store(to_bm)


def _inproj_rglru(x, g_pre, w_in, conv_w, conv_b, w_rg_a, b_rg_a, w_rg_x, b_rg_x, lru_lambda, ts):
    nb, s, d = x.shape
    tm = nb * ts
    nt = s // tm
    group = MXU_DIM // RG_HEAD_DIM
    n_blk = RG_HEADS // group
    wa = _block_diag(w_rg_a, group).astype(BF16)
    wx = _block_diag(w_rg_x, group).astype(BF16)
    row = lambda v: v.reshape(1, -1)
    const2 = lambda shape: pl.BlockSpec(shape, lambda j, b: (0, 0))
    const3 = lambda shape: pl.BlockSpec(shape, lambda j, b: (0, 0, 0))
    proj_idx = lambda j, b: (jnp.where(j < nt, b, nb - 1), jnp.minimum(j, nt - 1), 0)
    y_idx = lambda j, b: (0, jnp.where(j > 0, (j - 1) * nb + b, 0), 0)
    w_out = D_IN - N_RG_CHUNKS * PROJ_CHUNK
    return pl.pallas_call(
        functools.partial(_inproj_rglru_kernel, nt=nt, nb=nb, ts=ts),
        grid=(nt + 1, nb),
        in_specs=[pl.BlockSpec((None, tm, d), proj_idx),
                  const2((1, d)),
                  _resident((d, D_IN), lambda j, b: (0, 0)),
                  const2((CONV_WIDTH, RG_W)), const2((1, RG_W)),
                  const3((n_blk, MXU_DIM, MXU_DIM)), const2((1, RG_W)),
                  const3((n_blk, MXU_DIM, MXU_DIM)), const2((1, RG_W)),
                  const2((1, RG_W))],
        out_specs=[pl.BlockSpec((None, tm, w_out), proj_idx),
                   pl.BlockSpec((nb, ts, RG_W), y_idx)],
        out_shape=[jax.ShapeDtypeStruct((nb, s, w_out), BF16),
                   jax.ShapeDtypeStruct((nb, s, RG_W), BF16)],
        scratch_shapes=[
            pltpu.VMEM((2, nb, tm // RG_SUB, RG_SUB, N_RG_CHUNKS * PROJ_CHUNK), BF16),
            pltpu.VMEM((ts + CONV_WIDTH - 1, nb, RG_W), F32),
            pltpu.VMEM((ts, nb, RG_W), F32),
            pltpu.VMEM((ts, nb, RG_W), F32),
            pltpu.VMEM((ts, nb, RG_W), F32),
            pltpu.VMEM((nb, RG_W), F32),
        ],
        compiler_params=_params(("arbitrary", "arbitrary")),
        name="inproj_rglru",
    )(x, row(g_pre), w_in, conv_w, row(conv_b), wa, row(b_rg_a), wx, row(b_rg_x), row(lru_lambda))


def _attn_kernel(q_ref, k_ref, v_ref, g_ref, xq_ref, xg_ref, mem_ref, gm_ref, wm_ref,
                 lq1_ref, lk1_ref, lq2_ref, lk2_ref, gs_ref,
                 yda_ref, yxa_ref, vt_ref, km_ref, vmt_ref, *, tq, hp):
    hg = pl.program_id(1)
    hd = DA_V_DIM
    dh = XA_HEAD_DIM
    s_len = q_ref.shape[0]
    n_mem = mem_ref.shape[0]
    nq = s_len // tq
    w = 2 * tq

    @pl.when(hg == 0)
    def _():
        mn = _rms(mem_ref[...], gm_ref[...]).astype(BF16)
        kv = jnp.dot(mn, wm_ref[...].astype(BF16), preferred_element_type=F32)
        for hh in range(XA_HEADS):
            km_ref[hh] = kv[:, hh * dh:(hh + 1) * dh].astype(BF16)
            lo = XA_W + hh * dh
            vmt_ref[hh, 0:dh, :] = kv[:, lo:lo + dh].T.astype(BF16)
            vmt_ref[hh, dh:dh + BF16_ROWS, :] = jnp.ones((BF16_ROWS, n_mem), BF16)

    heads = [slice(i * hd, (i + 1) * hd) for i in range(hp)]
    for i, sl in enumerate(heads):
        vt_ref[i, 0:hd, :] = v_ref[:, sl].astype(F32).T.astype(BF16)
        vt_ref[i, hd:hd + BF16_ROWS, :] = jnp.ones((BF16_ROWS, s_len), BF16)

    lam = (jnp.exp(jnp.sum(lq1_ref[...] * lk1_ref[...], axis=-1, keepdims=True))
           - jnp.exp(jnp.sum(lq2_ref[...] * lk2_ref[...], axis=-1, keepdims=True))
           + LAMBDA_INIT)
    lane = lax.broadcasted_iota(jnp.int32, (tq, hd), 1)
    key = lax.broadcasted_iota(jnp.int32, (tq, w), 0)
    qry = lax.broadcasted_iota(jnp.int32, (tq, w), 1) & (tq - 1)
    causal = key <= qry

    def softmax_t(s3, vt):
        n = s3.shape[0] * SUBLANES
        m = jnp.max(jnp.max(s3, axis=0), axis=0, keepdims=True)
        p = jnp.exp2(s3 - m[None]).reshape(n, s3.shape[2]).astype(BF16)
        return jnp.dot(vt, p, preferred_element_type=F32)

    def da_scores(i, qb):
        kv = (qb + 1) * tq
        q = q_ref[qb * tq:kv, heads[i]].astype(F32)
        qq = jnp.concatenate([jnp.where(lane < DA_HEAD_DIM, q, 0.0),
                              jnp.where(lane >= DA_HEAD_DIM, q, 0.0)], axis=0)
        qqt = qq.T.astype(BF16)
        s_diag = jnp.dot(k_ref[kv - tq:kv, heads[i]], qqt, preferred_element_type=F32)
        parts = [jnp.where(causal, s_diag, -jnp.inf)]
        if qb > 0:
            parts.insert(0, jnp.dot(k_ref[0:kv - tq, heads[i]], qqt, preferred_element_type=F32))
        return jnp.concatenate(parts, axis=0).reshape(kv // SUBLANES, SUBLANES, w)

    def da_finish(i, qb, s3):
        rows = slice(qb * tq, (qb + 1) * tq)
        pv = softmax_t(s3, vt_ref[i, :, 0:(qb + 1) * tq])
        ot = pv[0:hd] / pv[hd:hd + 1]
        od = (ot[:, 0:tq] - lam * ot[:, tq:w]).T
        o = _rms(od, gs_ref[...]) * (1.0 - LAMBDA_INIT)
        yda_ref[rows, heads[i]] = (o * _silu(g_ref[rows, heads[i]].astype(F32))).astype(BF16)

    def xa_scores(i, _):
        km = km_ref[hg * hp + i]
        return _nt_dot(km, xq_ref[:, heads[i]]).reshape(n_mem // SUBLANES, SUBLANES, s_len)

    def xa_finish(i, _, s3):
        pv = softmax_t(s3, vmt_ref[hg * hp + i])
        o = (pv[0:dh] / pv[dh:dh + 1]).T
        yxa_ref[:, heads[i]] = (o * _silu(xg_ref[:, heads[i]].astype(F32))).astype(BF16)

    stages = [(da_scores, da_finish, 0), (xa_scores, xa_finish, None)]
    stages += [(da_scores, da_finish, qb) for qb in range(nq - 1, 0, -1)]
    chains = [(sc, fin, i, arg) for sc, fin, arg in stages for i in range(hp)]
    ahead = 2
    pending = [sc(i, arg) for sc, _, i, arg in chains[:ahead]]
    for pos, (_, finish, i, arg) in enumerate(chains):
        s3 = pending.pop(0)
        if pos + ahead < len(chains):
            sc, _, ni, nxt = chains[pos + ahead]
            pending.append(sc(ni, nxt))
        finish(i, arg, s3)


def _attn(proj, mem, g_mem, w_mem_kv, lq1, lk1, lq2, lk2, g_subln, tq, hp):
    b, s, _ = proj.shape
    m, d = mem.shape[1:]
    hd = DA_V_DIM
    assert XA_HEAD_DIM == hd and XA_HEADS == DA_HEADS and DA_HEADS % hp == 0
    groups = PROJ_CHUNK // (hp * hd)
    vec = lambda v: v.reshape(1, -1)
    cvec = lambda n: pl.BlockSpec((1, n), lambda bi, h: (0, 0))
    col = lambda chunk: pl.BlockSpec((None, s, hp * hd),
                                     lambda bi, h: (bi, 0, (chunk - N_RG_CHUNKS) * groups + h))
    out = pl.BlockSpec((None, s, hp * hd), lambda bi, h: (bi, 0, h))
    return pl.pallas_call(
        functools.partial(_attn_kernel, tq=tq, hp=hp),
        grid=(b, DA_HEADS // hp),
        in_specs=[col(C_DA_Q), col(C_DA_K), col(C_DA_V), col(C_DA_G), col(C_XA_Q), col(C_XA_G),
                  pl.BlockSpec((None, m, d), lambda bi, h: (bi, 0, 0)),
                  cvec(d),
                  _resident((d, 2 * XA_W), lambda bi, h: (0, 0)),
                  cvec(DA_HEAD_DIM), cvec(DA_HEAD_DIM), cvec(DA_HEAD_DIM), cvec(DA_HEAD_DIM),
                  cvec(DA_V_DIM)],
        out_specs=[out, out],
        out_shape=[jax.ShapeDtypeStruct((b, s, DA_W), BF16), jax.ShapeDtypeStruct((b, s, XA_W), BF16)],
        scratch_shapes=[pltpu.VMEM((hp, hd + BF16_ROWS, s), BF16),
                        pltpu.VMEM((XA_HEADS, m, hd), BF16),
                        pltpu.VMEM((XA_HEADS, hd + BF16_ROWS, m), BF16)],
        compiler_params=_params(("arbitrary", "arbitrary")),
        name="attn",
    )(proj, proj, proj, proj, proj, proj, mem, vec(g_mem), w_mem_kv,
      vec(lq1), vec(lk1), vec(lq2), vec(lk2), vec(g_subln))


def _outproj_kernel(yrg_ref, yda_ref, yxa_ref, w_ref, x_ref, g_ref, o_ref):
    y = jnp.dot(yrg_ref[...], w_ref[0:RG_W, :].astype(BF16), preferred_element_type=F32)
    y = y + jnp.dot(yda_ref[...], w_ref[RG_W:RG_W + DA_W, :].astype(BF16), preferred_element_type=F32)
    y = y + jnp.dot(yxa_ref[...], w_ref[RG_W + DA_W:, :].astype(BF16), preferred_element_type=F32)
    o_ref[...] = x_ref[...] + _rms(y, g_ref[...])


def _outproj(y_rg, y_da, y_xa, w_out, x, g_post, tm):
    b, s, d = x.shape
    d_mix = w_out.shape[0]
    tile = lambda width: pl.BlockSpec((None, tm, width), lambda i, j: (i, j, 0))
    return pl.pallas_call(
        _outproj_kernel,
        grid=(b, s // tm),
        in_specs=[
            tile(RG_W), tile(DA_W), tile(XA_W),
            _resident((d_mix, d), lambda i, j: (0, 0)),
            tile(d),
            pl.BlockSpec((1, d), lambda i, j: (0, 0)),
        ],
        out_specs=tile(d),
        out_shape=jax.ShapeDtypeStruct((b, s, d), F32),
        compiler_params=_params(("arbitrary", "arbitrary")),
        name="outproj",
    )(y_rg, y_da, y_xa, w_out, x, g_post.reshape(1, d))


def kernel(x, mem, g_pre, g_mem, w_in, w_mem_kv, conv_w, conv_b, w_rg_a, b_rg_a, w_rg_x, b_rg_x,
           lru_lambda, lambda_q1, lambda_k1, lambda_q2, lambda_k2, g_subln, w_out, g_post):
    proj, y_rg = _inproj_rglru(x, g_pre, w_in, conv_w, conv_b, w_rg_a, b_rg_a, w_rg_x, b_rg_x,
                               lru_lambda, ts=64)
    y_da, y_xa = _attn(proj, mem, g_mem, w_mem_kv,
                       lambda_q1, lambda_k1, lambda_q2, lambda_k2, g_subln, tq=256, hp=2)
    return _outproj(y_rg, y_da, y_xa, w_out, x, g_post, tm=1024)
```

```python
import functools
import math

import jax
import jax.numpy as jnp
from jax import lax
from jax.experimental import pallas as pl
from jax.experimental.pallas import tpu as pltpu

F32 = jnp.float32
BF16 = jnp.bfloat16

RMS_EPS = 1e-6
N_MEM = 256
RG_HEADS = 8
RG_HEAD_DIM = 64
RG_W = RG_HEADS * RG_HEAD_DIM
CONV_WIDTH = 4
LRU_C = 8.0
DA_HEADS = 4
DA_HEAD_DIM = 64
DA_V_DIM = 2 * DA_HEAD_DIM
DA_W = DA_HEADS * DA_V_DIM
LAMBDA_INIT = 0.8 - 0.6 * math.exp(-0.3 * (1 - 1))
XA_HEADS = 4
XA_HEAD_DIM = 128
XA_W = XA_HEADS * XA_HEAD_DIM
LOG2E = 1.4426950408889634

PROJ_CHUNK = 512
(C_RG_X, C_RG_G, C_DA_Q, C_DA_K, C_DA_V, C_DA_G, C_XA_Q, C_XA_G) = range(8)
D_IN = 8 * PROJ_CHUNK

SUBLANES = 8
LANES = 128
BF16_ROWS = 16
MXU_DIM = 256
VMEM_LIMIT_BYTES = 56 * 1024 * 1024


def _resident(shape, index_map):
    return pl.BlockSpec(shape, index_map, pipeline_mode=pl.Buffered(1))


def _params(sem):
    return pltpu.CompilerParams(dimension_semantics=sem, vmem_limit_bytes=VMEM_LIMIT_BYTES)


def _rms(x, g):
    ms = jnp.mean(x * x, axis=-1, keepdims=True)
    return x * lax.rsqrt(ms + RMS_EPS) * g


def _silu(x):
    h = 0.5 * x
    return h * (1.0 + jnp.tanh(h))


def _nt_dot(a, b):
    return lax.dot_general(a, b, (((1,), (1,)), ((), ())), preferred_element_type=F32)


RG_SUB = MXU_DIM // SUBLANES
N_RG_CHUNKS = 2


def _block_diag(w, group):
    h, di, dj = w.shape
    w = w.reshape(h // group, group, di, dj)
    eye = jnp.eye(group, dtype=w.dtype)
    return (eye[None, :, None, :, None] * w[:, :, :, None, :]).reshape(h // group, group * di, group * dj)


def _inproj_rglru_kernel(x_ref, g_ref, w_ref, cw_ref, cb_ref, wa_ref, ba_ref, wx_ref, bx_ref, lam_ref,
                         proj_ref, y_ref, rg_scr, xp_ref, g_scr, a_ref, u_ref, h_ref, *, nt, nb, ts):
    j = pl.program_id(0)
    b = pl.program_id(1)
    halo = CONV_WIDTH - 1
    sub = RG_SUB
    rows = sub * nb
    tm = x_ref.shape[0]
    n_sub = ts // sub
    assert rows == MXU_DIM and nb == SUBLANES and ts % sub == 0 and tm == nb * ts
    q_scale = {C_DA_Q: (DA_HEAD_DIM ** -0.5) * LOG2E, C_XA_Q: (XA_HEAD_DIM ** -0.5) * LOG2E}

    def normed():
        return _rms(x_ref[...], g_ref[...]).astype(BF16)

    def project(hn, chunks):
        for n in chunks:
            cols = slice(n * PROJ_CHUNK, (n + 1) * PROJ_CHUNK)
            acc = jnp.dot(hn, w_ref[:, cols].astype(BF16), preferred_element_type=F32)
            if n in q_scale:
                acc = acc * q_scale[n]
            if n < N_RG_CHUNKS:
                rg_scr[j & 1, b, :, :, cols] = acc.astype(BF16).reshape(tm // sub, sub, PROJ_CHUNK)
            else:
                out_cols = slice((n - N_RG_CHUNKS) * PROJ_CHUNK, (n - N_RG_CHUNKS + 1) * PROJ_CHUNK)
                proj_ref[:, out_cols] = acc.astype(BF16)

    def permutations():
        r = lax.broadcasted_iota(jnp.int32, (rows, rows), 0)
        c = lax.broadcasted_iota(jnp.int32, (rows, rows), 1)
        log_nb, log_sub = nb.bit_length() - 1, sub.bit_length() - 1
        to_time_major = (c == (r & (nb - 1)) * sub + (r >> log_nb)).astype(BF16)
        to_batch_major = (c == (r & (sub - 1)) * nb + (r >> log_sub)).astype(BF16)
        return to_time_major, to_batch_major

    def rg_load(to_time_major):
        xp_ref[0:halo] = xp_ref[ts:ts + halo]
        prev = (j + 1) & 1
        for sb in range(n_sub):
            t0 = sb * sub
            blk = rg_scr[prev, :, b * n_sub + sb].reshape(rows, 2 * RG_W)
            tmaj = jnp.dot(to_time_major, blk, preferred_element_type=F32).reshape(sub, nb, 2 * RG_W)
            xp_ref[halo + t0:halo + t0 + sub] = tmaj[:, :, 0:RG_W]
            g_scr[t0:t0 + sub] = tmaj[:, :, RG_W:2 * RG_W]

    def rg_recur():
        xh = 0.5 * cb_ref[...].reshape(1, 1, RG_W)
        for k in range(CONV_WIDTH):
            xh = xh + (0.5 * cw_ref[k:k + 1, :]).reshape(1, 1, RG_W) * xp_ref[k:k + ts]
        xb = xh.reshape(ts * nb, RG_W).astype(BF16)

        def gate(wg_ref, bg_ref):
            z = [jnp.dot(xb[:, m * MXU_DIM:(m + 1) * MXU_DIM], wg_ref[m], preferred_element_type=F32)
                 for m in range(RG_W // MXU_DIM)]
            return jnp.tanh(jnp.concatenate(z, axis=1) + 0.5 * bg_ref[...]).reshape(ts, nb, RG_W)

        tr = gate(wa_ref, ba_ref)
        ti = gate(wx_ref, bx_ref)
        z = -lam_ref[...]
        softplus = jnp.maximum(z, 0.0) + jnp.log1p(jnp.exp(-jnp.abs(z)))
        half_rate = ((-0.5 * LRU_C * LOG2E) * softplus).reshape(1, 1, RG_W)
        a = jnp.exp2(tr * half_rate + half_rate)
        d = jnp.maximum(1.0 - a * a, 0.0)
        mult = jnp.where(d > 0.0, d * lax.rsqrt(d), 0.0)
        a_ref[...] = a
        u_ref[...] = mult * (xh * (1.0 + ti))
        h = h_ref[...]
        for t in range(ts):
            h = a_ref[t] * h + u_ref[t]
            u_ref[t] = h
        h_ref[...] = h

    def rg_store(to_batch_major):
        for sb in range(n_sub):
            t0 = sb * sub
            y = (u_ref[t0:t0 + sub] * _silu(g_scr[t0:t0 + sub])).reshape(rows, RG_W).astype(BF16)
            bm = jnp.dot(to_batch_major, y, preferred_element_type=F32)
            y_ref[:, t0:t0 + sub, :] = bm.reshape(nb, sub, RG_W).astype(BF16)

    n_chunks = D_IN // PROJ_CHUNK

    @pl.when(j == 0)
    def _():
        xp_ref[...] = jnp.zeros(xp_ref.shape, F32)
        h_ref[...] = jnp.zeros(h_ref.shape, F32)
        project(normed(), range(n_chunks))

    @pl.when(jnp.logical_and(j > 0, j < nt))
    def _():
        to_tm, to_bm = permutations()
        hn = normed()
        rg_load(to_tm)
        project(hn, range(0, n_chunks // 2))
        rg_recur()
        project(hn, range(n_chunks // 2, n_chunks))
        rg_store(to_bm)

    @pl.when(j == nt)
    def _():
        to_tm, to_bm = permutations()
        rg_load(to_tm)
        rg_recur()
        rg_store(to_bm)


def _inproj_rglru(x, g_pre, w_in, conv_w, conv_b, w_rg_a, b_rg_a, w_rg_x, b_rg_x, lru_lambda, ts):
    nb, s, d = x.shape
    tm = nb * ts
    nt = s // tm
    group = MXU_DIM // RG_HEAD_DIM
    n_blk = RG_HEADS // group
    wa = _block_diag(w_rg_a, group).astype(BF16)
    wx = _block_diag(w_rg_x, group).astype(BF16)
    row = lambda v: v.reshape(1, -1)
    const2 = lambda shape: pl.BlockSpec(shape, lambda j, b: (0, 0))
    const3 = lambda shape: pl.BlockSpec(shape, lambda j, b: (0, 0, 0))
    proj_idx = lambda j, b: (jnp.where(j < nt, b, nb - 1), jnp.minimum(j, nt - 1), 0)
    y_idx = lambda j, b: (0, jnp.where(j > 0, (j - 1) * nb + b, 0), 0)
    w_out = D_IN - N_RG_CHUNKS * PROJ_CHUNK
    return pl.pallas_call(
        functools.partial(_inproj_rglru_kernel, nt=nt, nb=nb, ts=ts),
        grid=(nt + 1, nb),
        in_specs=[pl.BlockSpec((None, tm, d), proj_idx),
                  const2((1, d)),
                  _resident((d, D_IN), lambda j, b: (0, 0)),
                  const2((CONV_WIDTH, RG_W)), const2((1, RG_W)),
                  const3((n_blk, MXU_DIM, MXU_DIM)), const2((1, RG_W)),
                  const3((n_blk, MXU_DIM, MXU_DIM)), const2((1, RG_W)),
                  const2((1, RG_W))],
        out_specs=[pl.BlockSpec((None, tm, w_out), proj_idx),
                   pl.BlockSpec((nb, ts, RG_W), y_idx)],
        out_shape=[jax.ShapeDtypeStruct((nb, s, w_out), BF16),
                   jax.ShapeDtypeStruct((nb, s, RG_W), BF16)],
        scratch_shapes=[
            pltpu.VMEM((2, nb, tm // RG_SUB, RG_SUB, N_RG_CHUNKS * PROJ_CHUNK), BF16),
            pltpu.VMEM((ts + CONV_WIDTH - 1, nb, RG_W), F32),
            pltpu.VMEM((ts, nb, RG_W), F32),
            pltpu.VMEM((ts, nb, RG_W), F32),
            pltpu.VMEM((ts, nb, RG_W), F32),
            pltpu.VMEM((nb, RG_W), F32),
        ],
        compiler_params=_params(("arbitrary", "arbitrary")),
        name="inproj_rglru",
    )(x, row(g_pre), w_in, conv_w, row(conv_b), wa, row(b_rg_a), wx, row(b_rg_x), row(lru_lambda))


def _attn_kernel(q_ref, k_ref, v_ref, g_ref, xq_ref, xg_ref, mem_ref, gm_ref, wm_ref,
                 lq1_ref, lk1_ref, lq2_ref, lk2_ref, gs_ref,
                 yda_ref, yxa_ref, vt_ref, km_ref, vmt_ref, s_scr, xs_scr, *, tq, hp):
    hg = pl.program_id(1)
    hd = DA_V_DIM
    dh = XA_HEAD_DIM
    s_len = q_ref.shape[0]
    n_mem = mem_ref.shape[0]
    nq = s_len // tq
    w = 2 * tq

    @pl.when(hg == 0)
    def _():
        mn = _rms(mem_ref[...], gm_ref[...]).astype(BF16)
        kv = jnp.dot(mn, wm_ref[...].astype(BF16), preferred_element_type=F32)
        for hh in range(XA_HEADS):
            km_ref[hh] = kv[:, hh * dh:(hh + 1) * dh].astype(BF16)
            lo = XA_W + hh * dh
            vmt_ref[hh, 0:dh, :] = kv[:, lo:lo + dh].T.astype(BF16)
            vmt_ref[hh, dh:dh + BF16_ROWS, :] = jnp.ones((BF16_ROWS, n_mem), BF16)

    heads = [slice(i * hd, (i + 1) * hd) for i in range(hp)]
    for i, sl in enumerate(heads):
        vt_ref[i, 0:hd, :] = v_ref[:, sl].astype(F32).T.astype(BF16)
        vt_ref[i, hd:hd + BF16_ROWS, :] = jnp.ones((BF16_ROWS, s_len), BF16)

    lam = (jnp.exp(jnp.sum(lq1_ref[...] * lk1_ref[...], axis=-1, keepdims=True))
           - jnp.exp(jnp.sum(lq2_ref[...] * lk2_ref[...], axis=-1, keepdims=True))
           + LAMBDA_INIT)
    lane = lax.broadcasted_iota(jnp.int32, (tq, hd), 1)
    key = lax.broadcasted_iota(jnp.int32, (tq, w), 0)
    qry = lax.broadcasted_iota(jnp.int32, (tq, w), 1) & (tq - 1)
    causal = key <= qry

    def softmax_t(s3, vt):
        n = s3.shape[0] * SUBLANES
        m = jnp.max(jnp.max(s3, axis=0), axis=0, keepdims=True)
        p = jnp.exp2(s3 - m[None]).reshape(n, s3.shape[2]).astype(BF16)
        return jnp.dot(vt, p, preferred_element_type=F32)

    def da_scores(i, qb, slot):
        kv = (qb + 1) * tq
        n_off = (kv - tq) // SUBLANES
        q = q_ref[qb * tq:kv, heads[i]].astype(F32)
        qq = jnp.concatenate([jnp.where(lane < DA_HEAD_DIM, q, 0.0),
                              jnp.where(lane >= DA_HEAD_DIM, q, 0.0)], axis=0)
        qqt = qq.T.astype(BF16)
        s_diag = jnp.dot(k_ref[kv - tq:kv, heads[i]], qqt, preferred_element_type=F32)
        s_scr[slot, n_off:kv // SUBLANES] = jnp.where(causal, s_diag, -jnp.inf).reshape(
            tq // SUBLANES, SUBLANES, w)
        if qb > 0:
            s_off = jnp.dot(k_ref[0:kv - tq, heads[i]], qqt, preferred_element_type=F32)
            s_scr[slot, 0:n_off] = s_off.reshape(n_off, SUBLANES, w)

    def da_finish(i, qb, slot):
        rows = slice(qb * tq, (qb + 1) * tq)
        s3 = s_scr[slot, 0:(qb + 1) * tq // SUBLANES]
        pv = softmax_t(s3, vt_ref[i, :, 0:(qb + 1) * tq])
        ot = pv[0:hd] / pv[hd:hd + 1]
        od = (ot[:, 0:tq] - lam * ot[:, tq:w]).T
        o = _rms(od, gs_ref[...]) * (1.0 - LAMBDA_INIT)
        yda_ref[rows, heads[i]] = (o * _silu(g_ref[rows, heads[i]].astype(F32))).astype(BF16)

    def xa_scores(i, _, slot):
        km = km_ref[hg * hp + i]
        xs_scr[i] = _nt_dot(km, xq_ref[:, heads[i]]).reshape(n_mem // SUBLANES, SUBLANES, s_len)

    def xa_finish(i, _, slot):
        pv = softmax_t(xs_scr[i], vmt_ref[hg * hp + i])
        o = (pv[0:dh] / pv[dh:dh + 1]).T
        yxa_ref[:, heads[i]] = (o * _silu(xg_ref[:, heads[i]].astype(F32))).astype(BF16)

    stages = [(xa_scores, xa_finish, None)]
    stages += [(da_scores, da_finish, qb) for qb in range(nq - 1, -1, -1)]
    chains = [(sc, fin, i, arg) for sc, fin, arg in stages for i in range(hp)]
    ahead = s_scr.shape[0] - 1
    for pos, (sc, _, i, arg) in enumerate(chains[:ahead]):
        sc(i, arg, pos % (ahead + 1))
    for pos, (_, finish, i, arg) in enumerate(chains):
        if pos + ahead < len(chains):
            sc, _, ni, nxt = chains[pos + ahead]
            sc(ni, nxt, (pos + ahead) % (ahead + 1))
        finish(i, arg, pos % (ahead + 1))


def _attn(proj, mem, g_mem, w_mem_kv, lq1, lk1, lq2, lk2, g_subln, tq, hp):
    b, s, _ = proj.shape
    m, d = mem.shape[1:]
    hd = DA_V_DIM
    assert XA_HEAD_DIM == hd and XA_HEADS == DA_HEADS and DA_HEADS % hp == 0
    groups = PROJ_CHUNK // (hp * hd)
    vec = lambda v: v.reshape(1, -1)
    cvec = lambda n: pl.BlockSpec((1, n), lambda bi, h: (0, 0))
    col = lambda chunk: pl.BlockSpec((None, s, hp * hd),
                                     lambda bi, h: (bi, 0, (chunk - N_RG_CHUNKS) * groups + h))
    out = pl.BlockSpec((None, s, hp * hd), lambda bi, h: (bi, 0, h))
    return pl.pallas_call(
        functools.partial(_attn_kernel, tq=tq, hp=hp),
        grid=(b, DA_HEADS // hp),
        in_specs=[col(C_DA_Q), col(C_DA_K), col(C_DA_V), col(C_DA_G), col(C_XA_Q), col(C_XA_G),
                  pl.BlockSpec((None, m, d), lambda bi, h: (bi, 0, 0)),
                  cvec(d),
                  _resident((d, 2 * XA_W), lambda bi, h: (0, 0)),
                  cvec(DA_HEAD_DIM), cvec(DA_HEAD_DIM), cvec(DA_HEAD_DIM), cvec(DA_HEAD_DIM),
                  cvec(DA_V_DIM)],
        out_specs=[out, out],
        out_shape=[jax.ShapeDtypeStruct((b, s, DA_W), BF16), jax.ShapeDtypeStruct((b, s, XA_W), BF16)],
        scratch_shapes=[pltpu.VMEM((hp, hd + BF16_ROWS, s), BF16),
                        pltpu.VMEM((XA_HEADS, m, hd), BF16),
                        pltpu.VMEM((XA_HEADS, hd + BF16_ROWS, m), BF16),
                        pltpu.VMEM((3, s // SUBLANES, SUBLANES, 2 * tq), F32),
                        pltpu.VMEM((hp, m // SUBLANES, SUBLANES, s), F32)],
        compiler_params=_params(("arbitrary", "arbitrary")),
        name="attn",
    )(proj, proj, proj, proj, proj, proj, mem, vec(g_mem), w_mem_kv,
      vec(lq1), vec(lk1), vec(lq2), vec(lk2), vec(g_subln))


def _outproj_kernel(yrg_ref, yda_ref, yxa_ref, w_ref, x_ref, g_ref, o_ref):
    w_rg = w_ref[0:RG_W, :].astype(BF16)
    w_da = w_ref[RG_W:RG_W + DA_W, :].astype(BF16)
    w_xa = w_ref[RG_W + DA_W:, :].astype(BF16)
    n_groups = 4
    rows_per = x_ref.shape[0] // n_groups
    for r in range(n_groups):
        rows = slice(r * rows_per, (r + 1) * rows_per)
        y = jnp.dot(yrg_ref[rows, :], w_rg, preferred_element_type=F32)
        y = y + jnp.dot(yda_ref[rows, :], w_da, preferred_element_type=F32)
        y = y + jnp.dot(yxa_ref[rows, :], w_xa, preferred_element_type=F32)
        o_ref[rows, :] = x_ref[rows, :] + _rms(y, g_ref[...])


def _outproj(y_rg, y_da, y_xa, w_out, x, g_post, tm):
    b, s, d = x.shape
    d_mix = w_out.shape[0]
    tile = lambda width: pl.BlockSpec((None, tm, width), lambda i, j: (i, j, 0))
    return pl.pallas_call(
        _outproj_kernel,
        grid=(b, s // tm),
        in_specs=[
            tile(RG_W), tile(DA_W), tile(XA_W),
            _resident((d_mix, d), lambda i, j: (0, 0)),
            tile(d),
            pl.BlockSpec((1, d), lambda i, j: (0, 0)),
        ],
        out_specs=tile(d),
        out_shape=jax.ShapeDtypeStruct((b, s, d), F32),
        compiler_params=_params(("arbitrary", "arbitrary")),
        name="outproj",
    )(y_rg, y_da, y_xa, w_out, x, g_post.reshape(1, d))


def kernel(x, mem, g_pre, g_mem, w_in, w_mem_kv, conv_w, conv_b, w_rg_a, b_rg_a, w_rg_x, b_rg_x,
           lru_lambda, lambda_q1, lambda_k1, lambda_q2, lambda_k2, g_subln, w_out, g_post):
    proj, y_rg = _inproj_rglru(x, g_pre, w_in, conv_w, conv_b, w_rg_a, b_rg_a, w_rg_x, b_rg_x,
                               lru_lambda, ts=64)
    y_da, y_xa = _attn(proj, mem, g_mem, w_mem_kv,
                       lambda_q1, lambda_k1, lambda_q2, lambda_k2, g_subln, tq=256, hp=1)
    return _outproj(y_rg, y_da, y_xa, w_out, x, g_post, tm=1024)
```

```python
import functools
import math

import jax
import jax.numpy as jnp
from jax import lax
from jax.experimental import pallas as pl
from jax.experimental.pallas import tpu as pltpu

F32 = jnp.float32
BF16 = jnp.bfloat16

RMS_EPS = 1e-6
N_MEM = 256
RG_HEADS = 8
RG_HEAD_DIM = 64
RG_W = RG_HEADS * RG_HEAD_DIM
CONV_WIDTH = 4
LRU_C = 8.0
DA_HEADS = 4
DA_HEAD_DIM = 64
DA_V_DIM = 2 * DA_HEAD_DIM
DA_W = DA_HEADS * DA_V_DIM
LAMBDA_INIT = 0.8 - 0.6 * math.exp(-0.3 * (1 - 1))
XA_HEADS = 4
XA_HEAD_DIM = 128
XA_W = XA_HEADS * XA_HEAD_DIM
LOG2E = 1.4426950408889634

PROJ_CHUNK = 512
(C_RG_X, C_RG_G, C_DA_Q, C_DA_K, C_DA_V, C_DA_G, C_XA_Q, C_XA_G) = range(8)
D_IN = 8 * PROJ_CHUNK

SUBLANES = 8
LANES = 128
BF16_ROWS = 16
MXU_DIM = 256
VMEM_LIMIT_BYTES = 56 * 1024 * 1024


def _resident(shape, index_map):
    return pl.BlockSpec(shape, index_map, pipeline_mode=pl.Buffered(1))


def _params(sem):
    return pltpu.CompilerParams(dimension_semantics=sem, vmem_limit_bytes=VMEM_LIMIT_BYTES)


def _rms(x, g):
    ms = jnp.mean(x * x, axis=-1, keepdims=True)
    return x * lax.rsqrt(ms + RMS_EPS) * g


def _silu(x):
    h = 0.5 * x
    return h * (1.0 + jnp.tanh(h))


def _nt_dot(a, b):
    return lax.dot_general(a, b, (((1,), (1,)), ((), ())), preferred_element_type=F32)


RG_SUB = MXU_DIM // SUBLANES
N_RG_CHUNKS = 2


def _block_diag(w, group):
    h, di, dj = w.shape
    w = w.reshape(h // group, group, di, dj)
    eye = jnp.eye(group, dtype=w.dtype)
    return (eye[None, :, None, :, None] * w[:, :, :, None, :]).reshape(h // group, group * di, group * dj)


def _inproj_rglru_kernel(x_ref, g_ref, w_ref, cw_ref, cb_ref, wa_ref, ba_ref, wx_ref, bx_ref, lam_ref,
                         proj_ref, y_ref, rg_scr, xp_ref, g_scr, a_ref, u_ref, h_ref, *, nt, nb, ts):
    j = pl.program_id(0)
    halo = CONV_WIDTH - 1
    sub = RG_SUB
    rows = sub * nb
    n_sub = ts // sub
    d = x_ref.shape[-1]
    assert rows == MXU_DIM and nb == SUBLANES and ts % sub == 0
    q_scale = {C_DA_Q: (DA_HEAD_DIM ** -0.5) * LOG2E, C_XA_Q: (XA_HEAD_DIM ** -0.5) * LOG2E}

    def normed():
        return _rms(x_ref[...].reshape(nb * ts, d), g_ref[...]).astype(BF16)

    def project(hn, chunks):
        for n in chunks:
            cols = slice(n * PROJ_CHUNK, (n + 1) * PROJ_CHUNK)
            acc = jnp.dot(hn, w_ref[:, cols].astype(BF16), preferred_element_type=F32)
            if n in q_scale:
                acc = acc * q_scale[n]
            acc = acc.astype(BF16).reshape(nb, ts, PROJ_CHUNK)
            if n < N_RG_CHUNKS:
                rg_scr[j & 1, :, :, cols] = acc
            else:
                out_cols = slice((n - N_RG_CHUNKS) * PROJ_CHUNK, (n - N_RG_CHUNKS + 1) * PROJ_CHUNK)
                proj_ref[:, :, out_cols] = acc

    def permutations():
        r = lax.broadcasted_iota(jnp.int32, (rows, rows), 0)
        c = lax.broadcasted_iota(jnp.int32, (rows, rows), 1)
        log_nb, log_sub = nb.bit_length() - 1, sub.bit_length() - 1
        to_time_major = (c == (r & (nb - 1)) * sub + (r >> log_nb)).astype(BF16)
        to_batch_major = (c == (r & (sub - 1)) * nb + (r >> log_sub)).astype(BF16)
        return to_time_major, to_batch_major

    def rg_load(to_time_major):
        xp_ref[0:halo] = xp_ref[ts:ts + halo]
        prev = (j + 1) & 1
        for sb in range(n_sub):
            t0 = sb * sub
            blk = rg_scr[prev, :, t0:t0 + sub, :].reshape(rows, 2 * RG_W)
            tmaj = jnp.dot(to_time_major, blk, preferred_element_type=F32).reshape(sub, nb, 2 * RG_W)
            xp_ref[halo + t0:halo + t0 + sub] = tmaj[:, :, 0:RG_W]
            g_scr[t0:t0 + sub] = tmaj[:, :, RG_W:2 * RG_W]

    def rg_recur():
        xh = 0.5 * cb_ref[...].reshape(1, 1, RG_W)
        for k in range(CONV_WIDTH):
            xh = xh + (0.5 * cw_ref[k:k + 1, :]).reshape(1, 1, RG_W) * xp_ref[k:k + ts]
        xb = xh.reshape(ts * nb, RG_W).astype(BF16)

        def gate(wg_ref, bg_ref):
            z = [jnp.dot(xb[:, m * MXU_DIM:(m + 1) * MXU_DIM], wg_ref[m], preferred_element_type=F32)
                 for m in range(RG_W // MXU_DIM)]
            return jnp.tanh(jnp.concatenate(z, axis=1) + 0.5 * bg_ref[...]).reshape(ts, nb, RG_W)

        tr = gate(wa_ref, ba_ref)
        ti = gate(wx_ref, bx_ref)
        z = -lam_ref[...]
        softplus = jnp.maximum(z, 0.0) + jnp.log1p(jnp.exp(-jnp.abs(z)))
        half_rate = ((-0.5 * LRU_C * LOG2E) * softplus).reshape(1, 1, RG_W)
        a = jnp.exp2(tr * half_rate + half_rate)
        d = jnp.maximum(1.0 - a * a, 0.0)
        mult = jnp.where(d > 0.0, d * lax.rsqrt(d), 0.0)
        a_ref[...] = a
        u_ref[...] = mult * (xh * (1.0 + ti))
        h = h_ref[...]
        for t in range(ts):
            h = a_ref[t] * h + u_ref[t]
            u_ref[t] = h
        h_ref[...] = h

    def rg_store(to_batch_major):
        for sb in range(n_sub):
            t0 = sb * sub
            y = (u_ref[t0:t0 + sub] * _silu(g_scr[t0:t0 + sub])).reshape(rows, RG_W).astype(BF16)
            bm = jnp.dot(to_batch_major, y, preferred_element_type=F32)
            y_ref[:, t0:t0 + sub, :] = bm.reshape(nb, sub, RG_W).astype(BF16)

    n_chunks = D_IN // PROJ_CHUNK

    @pl.when(j == 0)
    def _():
        xp_ref[...] = jnp.zeros(xp_ref.shape, F32)
        h_ref[...] = jnp.zeros(h_ref.shape, F32)
        project(normed(), range(n_chunks))

    @pl.when(jnp.logical_and(j > 0, j < nt))
    def _():
        to_tm, to_bm = permutations()
        hn = normed()
        rg_load(to_tm)
        project(hn, range(0, n_chunks // 2))
        rg_recur()
        project(hn, range(n_chunks // 2, n_chunks))
        rg_store(to_bm)

    @pl.when(j == nt)
    def _():
        to_tm, to_bm = permutations()
        rg_load(to_tm)
        rg_recur()
        rg_store(to_bm)


def _inproj_rglru(x, g_pre, w_in, conv_w, conv_b, w_rg_a, b_rg_a, w_rg_x, b_rg_x, lru_lambda, ts):
    nb, s, d = x.shape
    nt = s // ts
    group = MXU_DIM // RG_HEAD_DIM
    n_blk = RG_HEADS // group
    wa = _block_diag(w_rg_a, group).astype(BF16)
    wx = _block_diag(w_rg_x, group).astype(BF16)
    row = lambda v: v.reshape(1, -1)
    const2 = lambda shape: pl.BlockSpec(shape, lambda j: (0, 0))
    const3 = lambda shape: pl.BlockSpec(shape, lambda j: (0, 0, 0))
    proj_idx = lambda j: (0, jnp.minimum(j, nt - 1), 0)
    y_idx = lambda j: (0, jnp.maximum(j - 1, 0), 0)
    w_out = D_IN - N_RG_CHUNKS * PROJ_CHUNK
    return pl.pallas_call(
        functools.partial(_inproj_rglru_kernel, nt=nt, nb=nb, ts=ts),
        grid=(nt + 1,),
        in_specs=[pl.BlockSpec((nb, ts, d), proj_idx),
                  const2((1, d)),
                  _resident((d, D_IN), lambda j: (0, 0)),
                  const2((CONV_WIDTH, RG_W)), const2((1, RG_W)),
                  const3((n_blk, MXU_DIM, MXU_DIM)), const2((1, RG_W)),
                  const3((n_blk, MXU_DIM, MXU_DIM)), const2((1, RG_W)),
                  const2((1, RG_W))],
        out_specs=[pl.BlockSpec((nb, ts, w_out), proj_idx),
                   pl.BlockSpec((nb, ts, RG_W), y_idx)],
        out_shape=[jax.ShapeDtypeStruct((nb, s, w_out), BF16),
                   jax.ShapeDtypeStruct((nb, s, RG_W), BF16)],
        scratch_shapes=[
            pltpu.VMEM((2, nb, ts, N_RG_CHUNKS * PROJ_CHUNK), BF16),
            pltpu.VMEM((ts + CONV_WIDTH - 1, nb, RG_W), F32),
            pltpu.VMEM((ts, nb, RG_W), F32),
            pltpu.VMEM((ts, nb, RG_W), F32),
            pltpu.VMEM((ts, nb, RG_W), F32),
            pltpu.VMEM((nb, RG_W), F32),
        ],
        compiler_params=_params(("arbitrary",)),
        name="inproj_rglru",
    )(x, row(g_pre), w_in, conv_w, row(conv_b), wa, row(b_rg_a), wx, row(b_rg_x), row(lru_lambda))


def _attn_kernel(q_ref, k_ref, v_ref, g_ref, xq_ref, xg_ref, mem_ref, gm_ref, wm_ref,
                 lq1_ref, lk1_ref, lq2_ref, lk2_ref, gs_ref,
                 yda_ref, yxa_ref, vt_ref, km_ref, vmt_ref, s_scr, xs_scr, *, tq, hp):
    hg = pl.program_id(1)
    hd = DA_V_DIM
    dh = XA_HEAD_DIM
    s_len = q_ref.shape[0]
    n_mem = mem_ref.shape[0]
    nq = s_len // tq
    w = 2 * tq

    @pl.when(hg == 0)
    def _():
        mn = _rms(mem_ref[...], gm_ref[...]).astype(BF16)
        kv = jnp.dot(mn, wm_ref[...].astype(BF16), preferred_element_type=F32)
        for hh in range(XA_HEADS):
            km_ref[hh] = kv[:, hh * dh:(hh + 1) * dh].astype(BF16)
            lo = XA_W + hh * dh
            vmt_ref[hh, 0:dh, :] = kv[:, lo:lo + dh].T.astype(BF16)
            vmt_ref[hh, dh:dh + BF16_ROWS, :] = jnp.ones((BF16_ROWS, n_mem), BF16)

    heads = [slice(i * hd, (i + 1) * hd) for i in range(hp)]
    for i, sl in enumerate(heads):
        vt_ref[i, 0:hd, :] = v_ref[:, sl].astype(F32).T.astype(BF16)
        vt_ref[i, hd:hd + BF16_ROWS, :] = jnp.ones((BF16_ROWS, s_len), BF16)

    lam = (jnp.exp(jnp.sum(lq1_ref[...] * lk1_ref[...], axis=-1, keepdims=True))
           - jnp.exp(jnp.sum(lq2_ref[...] * lk2_ref[...], axis=-1, keepdims=True))
           + LAMBDA_INIT)
    lane = lax.broadcasted_iota(jnp.int32, (tq, hd), 1)
    key = lax.broadcasted_iota(jnp.int32, (tq, w), 0)
    qry = lax.broadcasted_iota(jnp.int32, (tq, w), 1) & (tq - 1)
    causal = key <= qry

    def softmax_t(s3, vt):
        n = s3.shape[0] * SUBLANES
        m = jnp.max(jnp.max(s3, axis=0), axis=0, keepdims=True)
        p = jnp.exp2(s3 - m[None]).reshape(n, s3.shape[2]).astype(BF16)
        return jnp.dot(vt, p, preferred_element_type=F32)

    def da_scores(i, qb, slot):
        kv = (qb + 1) * tq
        n_off = (kv - tq) // SUBLANES
        q = q_ref[qb * tq:kv, heads[i]].astype(F32)
        qq = jnp.concatenate([jnp.where(lane < DA_HEAD_DIM, q, 0.0),
                              jnp.where(lane >= DA_HEAD_DIM, q, 0.0)], axis=0)
        qqt = qq.T.astype(BF16)
        s_diag = jnp.dot(k_ref[kv - tq:kv, heads[i]], qqt, preferred_element_type=F32)
        s_scr[slot, n_off:kv // SUBLANES] = jnp.where(causal, s_diag, -jnp.inf).reshape(
            tq // SUBLANES, SUBLANES, w)
        if qb > 0:
            s_off = jnp.dot(k_ref[0:kv - tq, heads[i]], qqt, preferred_element_type=F32)
            s_scr[slot, 0:n_off] = s_off.reshape(n_off, SUBLANES, w)

    def da_finish(i, qb, slot):
        rows = slice(qb * tq, (qb + 1) * tq)
        s3 = s_scr[slot, 0:(qb + 1) * tq // SUBLANES]
        pv = softmax_t(s3, vt_ref[i, :, 0:(qb + 1) * tq])
        ot = pv[0:hd] / pv[hd:hd + 1]
        od = (ot[:, 0:tq] - lam * ot[:, tq:w]).T
        o = _rms(od, gs_ref[...]) * (1.0 - LAMBDA_INIT)
        yda_ref[rows, heads[i]] = (o * _silu(g_ref[rows, heads[i]].astype(F32))).astype(BF16)

    def xa_scores(i, _, slot):
        km = km_ref[hg * hp + i]
        xs_scr[i] = _nt_dot(km, xq_ref[:, heads[i]]).reshape(n_mem // SUBLANES, SUBLANES, s_len)

    def xa_finish(i, _, slot):
        pv = softmax_t(xs_scr[i], vmt_ref[hg * hp + i])
        o = (pv[0:dh] / pv[dh:dh + 1]).T
        yxa_ref[:, heads[i]] = (o * _silu(xg_ref[:, heads[i]].astype(F32))).astype(BF16)

    stages = [(xa_scores, xa_finish, None)]
    stages += [(da_scores, da_finish, qb) for qb in range(nq - 1, -1, -1)]
    chains = [(sc, fin, i, arg) for sc, fin, arg in stages for i in range(hp)]
    ahead = s_scr.shape[0] - 1
    for pos, (sc, _, i, arg) in enumerate(chains[:ahead]):
        sc(i, arg, pos % (ahead + 1))
    for pos, (_, finish, i, arg) in enumerate(chains):
        if pos + ahead < len(chains):
            sc, _, ni, nxt = chains[pos + ahead]
            sc(ni, nxt, (pos + ahead) % (ahead + 1))
        finish(i, arg, pos % (ahead + 1))


def _attn(proj, mem, g_mem, w_mem_kv, lq1, lk1, lq2, lk2, g_subln, tq, hp):
    b, s, _ = proj.shape
    m, d = mem.shape[1:]
    hd = DA_V_DIM
    assert XA_HEAD_DIM == hd and XA_HEADS == DA_HEADS and DA_HEADS % hp == 0
    groups = PROJ_CHUNK // (hp * hd)
    vec = lambda v: v.reshape(1, -1)
    cvec = lambda n: pl.BlockSpec((1, n), lambda bi, h: (0, 0))
    col = lambda chunk: pl.BlockSpec((None, s, hp * hd),
                                     lambda bi, h: (bi, 0, (chunk - N_RG_CHUNKS) * groups + h))
    out = pl.BlockSpec((None, s, hp * hd), lambda bi, h: (bi, 0, h))
    return pl.pallas_call(
        functools.partial(_attn_kernel, tq=tq, hp=hp),
        grid=(b, DA_HEADS // hp),
        in_specs=[col(C_DA_Q), col(C_DA_K), col(C_DA_V), col(C_DA_G), col(C_XA_Q), col(C_XA_G),
                  pl.BlockSpec((None, m, d), lambda bi, h: (bi, 0, 0)),
                  cvec(d),
                  _resident((d, 2 * XA_W), lambda bi, h: (0, 0)),
                  cvec(DA_HEAD_DIM), cvec(DA_HEAD_DIM), cvec(DA_HEAD_DIM), cvec(DA_HEAD_DIM),
                  cvec(DA_V_DIM)],
        out_specs=[out, out],
        out_shape=[jax.ShapeDtypeStruct((b, s, DA_W), BF16), jax.ShapeDtypeStruct((b, s, XA_W), BF16)],
        scratch_shapes=[pltpu.VMEM((hp, hd + BF16_ROWS, s), BF16),
                        pltpu.VMEM((XA_HEADS, m, hd), BF16),
                        pltpu.VMEM((XA_HEADS, hd + BF16_ROWS, m), BF16),
                        pltpu.VMEM((3, s // SUBLANES, SUBLANES, 2 * tq), F32),
                        pltpu.VMEM((hp, m // SUBLANES, SUBLANES, s), F32)],
        compiler_params=_params(("arbitrary", "arbitrary")),
        name="attn",
    )(proj, proj, proj, proj, proj, proj, mem, vec(g_mem), w_mem_kv,
      vec(lq1), vec(lk1), vec(lq2), vec(lk2), vec(g_subln))


def _outproj_kernel(yrg_ref, yda_ref, yxa_ref, w_ref, x_ref, g_ref, o_ref):
    w_rg = w_ref[0:RG_W, :].astype(BF16)
    w_da = w_ref[RG_W:RG_W + DA_W, :].astype(BF16)
    w_xa = w_ref[RG_W + DA_W:, :].astype(BF16)
    n_groups = 4
    rows_per = x_ref.shape[0] // n_groups
    for r in range(n_groups):
        rows = slice(r * rows_per, (r + 1) * rows_per)
        y = jnp.dot(yrg_ref[rows, :], w_rg, preferred_element_type=F32)
        y = y + jnp.dot(yda_ref[rows, :], w_da, preferred_element_type=F32)
        y = y + jnp.dot(yxa_ref[rows, :], w_xa, preferred_element_type=F32)
        o_ref[rows, :] = x_ref[rows, :] + _rms(y, g_ref[...])


def _outproj(y_rg, y_da, y_xa, w_out, x, g_post, tm):
    b, s, d = x.shape
    d_mix = w_out.shape[0]
    tile = lambda width: pl.BlockSpec((None, tm, width), lambda i, j: (i, j, 0))
    return pl.pallas_call(
        _outproj_kernel,
        grid=(b, s // tm),
        in_specs=[
            tile(RG_W), tile(DA_W), tile(XA_W),
            _resident((d_mix, d), lambda i, j: (0, 0)),
            tile(d),
            pl.BlockSpec((1, d), lambda i, j: (0, 0)),
        ],
        out_specs=tile(d),
        out_shape=jax.ShapeDtypeStruct((b, s, d), F32),
        compiler_params=_params(("arbitrary", "arbitrary")),
        name="outproj",
    )(y_rg, y_da, y_xa, w_out, x, g_post.reshape(1, d))


def kernel(x, mem, g_pre, g_mem, w_in, w_mem_kv, conv_w, conv_b, w_rg_a, b_rg_a, w_rg_x, b_rg_x,
           lru_lambda, lambda_q1, lambda_k1, lambda_q2, lambda_k2, g_subln, w_out, g_post):
    proj, y_rg = _inproj_rglru(x, g_pre, w_in, conv_w, conv_b, w_rg_a, b_rg_a, w_rg_x, b_rg_x,
                               lru_lambda, ts=128)
    y_da, y_xa = _attn(proj, mem, g_mem, w_mem_kv,
                       lambda_q1, lambda_k1, lambda_q2, lambda_k2, g_subln, tq=256, hp=1)
    return _outproj(y_rg, y_da, y_xa, w_out, x, g_post, tm=1024)
```

```python
import functools
import math

import jax
import jax.numpy as jnp
from jax import lax
from jax.experimental import pallas as pl
from jax.experimental.pallas import tpu as pltpu

F32 = jnp.float32
BF16 = jnp.bfloat16

RMS_EPS = 1e-6
N_MEM = 256
RG_HEADS = 8
RG_HEAD_DIM = 64
RG_W = RG_HEADS * RG_HEAD_DIM
CONV_WIDTH = 4
LRU_C = 8.0
DA_HEADS = 4
DA_HEAD_DIM = 64
DA_V_DIM = 2 * DA_HEAD_DIM
DA_W = DA_HEADS * DA_V_DIM
LAMBDA_INIT = 0.8 - 0.6 * math.exp(-0.3 * (1 - 1))
XA_HEADS = 4
XA_HEAD_DIM = 128
XA_W = XA_HEADS * XA_HEAD_DIM
LOG2E = 1.4426950408889634

PROJ_CHUNK = 512
(C_RG_X, C_RG_G, C_DA_Q, C_DA_K, C_DA_V, C_DA_G, C_XA_Q, C_XA_G) = range(8)
D_IN = 8 * PROJ_CHUNK

SUBLANES = 8
LANES = 128
BF16_ROWS = 16
MXU_DIM = 256
VMEM_LIMIT_BYTES = 56 * 1024 * 1024


def _resident(shape, index_map):
    return pl.BlockSpec(shape, index_map, pipeline_mode=pl.Buffered(1))


def _params(sem):
    return pltpu.CompilerParams(dimension_semantics=sem, vmem_limit_bytes=VMEM_LIMIT_BYTES)


def _rms(x, g):
    ms = jnp.mean(x * x, axis=-1, keepdims=True)
    return x * lax.rsqrt(ms + RMS_EPS) * g


def _silu(x):
    h = 0.5 * x
    return h * (1.0 + jnp.tanh(h))


def _nt_dot(a, b):
    return lax.dot_general(a, b, (((1,), (1,)), ((), ())), preferred_element_type=F32)


RG_SUB = MXU_DIM // SUBLANES
N_RG_CHUNKS = 2


def _block_diag(w, group):
    h, di, dj = w.shape
    w = w.reshape(h // group, group, di, dj)
    eye = jnp.eye(group, dtype=w.dtype)
    return (eye[None, :, None, :, None] * w[:, :, :, None, :]).reshape(h // group, group * di, group * dj)


def _inproj_rglru_kernel(x_ref, g_ref, w_ref, cw_ref, cb_ref, wa_ref, ba_ref, wx_ref, bx_ref, lam_ref,
                         proj_ref, y_ref, rg_scr, xp_ref, a_ref, u_ref, h_ref, *, nt, nb, ts):
    j = pl.program_id(0)
    halo = CONV_WIDTH - 1
    sub = RG_SUB
    rows = sub * nb
    n_sub = ts // sub
    d = x_ref.shape[-1]
    assert rows == MXU_DIM and nb == SUBLANES and ts % sub == 0
    q_scale = {C_DA_Q: (DA_HEAD_DIM ** -0.5) * LOG2E, C_XA_Q: (XA_HEAD_DIM ** -0.5) * LOG2E}

    def normed():
        return _rms(x_ref[...].reshape(nb * ts, d), g_ref[...]).astype(BF16)

    def project(hn, chunks):
        for n in chunks:
            cols = slice(n * PROJ_CHUNK, (n + 1) * PROJ_CHUNK)
            acc = jnp.dot(hn, w_ref[:, cols].astype(BF16), preferred_element_type=F32)
            if n in q_scale:
                acc = acc * q_scale[n]
            acc = acc.astype(BF16).reshape(nb, ts, PROJ_CHUNK)
            if n < N_RG_CHUNKS:
                rg_scr[j & 1, :, :, cols] = acc
            else:
                out_cols = slice((n - N_RG_CHUNKS) * PROJ_CHUNK, (n - N_RG_CHUNKS + 1) * PROJ_CHUNK)
                proj_ref[:, :, out_cols] = acc

    def permutations():
        r = lax.broadcasted_iota(jnp.int32, (rows, rows), 0)
        c = lax.broadcasted_iota(jnp.int32, (rows, rows), 1)
        log_nb, log_sub = nb.bit_length() - 1, sub.bit_length() - 1
        to_time_major = (c == (r & (nb - 1)) * sub + (r >> log_nb)).astype(BF16)
        to_batch_major = (c == (r & (sub - 1)) * nb + (r >> log_sub)).astype(BF16)
        return to_time_major, to_batch_major

    def rg_load(to_time_major):
        xp_ref[0:halo] = xp_ref[ts:ts + halo]
        prev = (j + 1) & 1
        for sb in range(n_sub):
            t0 = sb * sub
            blk = rg_scr[prev, :, t0:t0 + sub, 0:RG_W].reshape(rows, RG_W)
            tmaj = jnp.dot(to_time_major, blk, preferred_element_type=F32)
            xp_ref[halo + t0:halo + t0 + sub] = tmaj.reshape(sub, nb, RG_W)

    def rg_recur():
        xh = 0.5 * cb_ref[...].reshape(1, 1, RG_W)
        for k in range(CONV_WIDTH):
            xh = xh + (0.5 * cw_ref[k:k + 1, :]).reshape(1, 1, RG_W) * xp_ref[k:k + ts]
        xb = xh.reshape(ts * nb, RG_W).astype(BF16)

        def gate(wg_ref, bg_ref):
            z = [jnp.dot(xb[:, m * MXU_DIM:(m + 1) * MXU_DIM], wg_ref[m], preferred_element_type=F32)
                 for m in range(RG_W // MXU_DIM)]
            return jnp.tanh(jnp.concatenate(z, axis=1) + 0.5 * bg_ref[...]).reshape(ts, nb, RG_W)

        tr = gate(wa_ref, ba_ref)
        ti = gate(wx_ref, bx_ref)
        z = -lam_ref[...]
        softplus = jnp.maximum(z, 0.0) + jnp.log1p(jnp.exp(-jnp.abs(z)))
        half_rate = ((-0.5 * LRU_C * LOG2E) * softplus).reshape(1, 1, RG_W)
        a = jnp.exp2(tr * half_rate + half_rate)
        d = jnp.maximum(1.0 - a * a, 0.0)
        mult = jnp.where(d > 0.0, d * lax.rsqrt(d), 0.0)
        a_ref[...] = a
        u_ref[...] = mult * (xh * (1.0 + ti))
        h = h_ref[...]
        for t in range(ts):
            h = a_ref[t] * h + u_ref[t]
            u_ref[t] = h
        h_ref[...] = h

    def rg_store(to_batch_major):
        prev = (j + 1) & 1
        for sb in range(n_sub):
            t0 = sb * sub
            h_tm = u_ref[t0:t0 + sub].reshape(rows, RG_W).astype(BF16)
            h_bm = jnp.dot(to_batch_major, h_tm, preferred_element_type=F32).reshape(nb, sub, RG_W)
            g_bm = rg_scr[prev, :, t0:t0 + sub, RG_W:2 * RG_W].astype(F32)
            y_ref[:, t0:t0 + sub, :] = (h_bm * _silu(g_bm)).astype(BF16)

    n_chunks = D_IN // PROJ_CHUNK

    @pl.when(j == 0)
    def _():
        xp_ref[...] = jnp.zeros(xp_ref.shape, F32)
        h_ref[...] = jnp.zeros(h_ref.shape, F32)
        project(normed(), range(n_chunks))

    @pl.when(jnp.logical_and(j > 0, j < nt))
    def _():
        to_tm, to_bm = permutations()
        hn = normed()
        rg_load(to_tm)
        project(hn, range(0, n_chunks // 2))
        rg_recur()
        project(hn, range(n_chunks // 2, n_chunks))
        rg_store(to_bm)

    @pl.when(j == nt)
    def _():
        to_tm, to_bm = permutations()
        rg_load(to_tm)
        rg_recur()
        rg_store(to_bm)


def _inproj_rglru(x, g_pre, w_in, conv_w, conv_b, w_rg_a, b_rg_a, w_rg_x, b_rg_x, lru_lambda, ts):
    nb, s, d = x.shape
    nt = s // ts
    group = MXU_DIM // RG_HEAD_DIM
    n_blk = RG_HEADS // group
    wa = _block_diag(w_rg_a, group).astype(BF16)
    wx = _block_diag(w_rg_x, group).astype(BF16)
    row = lambda v: v.reshape(1, -1)
    const2 = lambda shape: pl.BlockSpec(shape, lambda j: (0, 0))
    const3 = lambda shape: pl.BlockSpec(shape, lambda j: (0, 0, 0))
    proj_idx = lambda j: (0, jnp.minimum(j, nt - 1), 0)
    y_idx = lambda j: (0, jnp.maximum(j - 1, 0), 0)
    w_out = D_IN - N_RG_CHUNKS * PROJ_CHUNK
    return pl.pallas_call(
        functools.partial(_inproj_rglru_kernel, nt=nt, nb=nb, ts=ts),
        grid=(nt + 1,),
        in_specs=[pl.BlockSpec((nb, ts, d), proj_idx),
                  const2((1, d)),
                  _resident((d, D_IN), lambda j: (0, 0)),
                  const2((CONV_WIDTH, RG_W)), const2((1, RG_W)),
                  const3((n_blk, MXU_DIM, MXU_DIM)), const2((1, RG_W)),
                  const3((n_blk, MXU_DIM, MXU_DIM)), const2((1, RG_W)),
                  const2((1, RG_W))],
        out_specs=[pl.BlockSpec((nb, ts, w_out), proj_idx),
                   pl.BlockSpec((nb, ts, RG_W), y_idx)],
        out_shape=[jax.ShapeDtypeStruct((nb, s, w_out), BF16),
                   jax.ShapeDtypeStruct((nb, s, RG_W), BF16)],
        scratch_shapes=[
            pltpu.VMEM((2, nb, ts, N_RG_CHUNKS * PROJ_CHUNK), BF16),
            pltpu.VMEM((ts + CONV_WIDTH - 1, nb, RG_W), F32),
            pltpu.VMEM((ts, nb, RG_W), F32),
            pltpu.VMEM((ts, nb, RG_W), F32),
            pltpu.VMEM((nb, RG_W), F32),
        ],
        compiler_params=_params(("arbitrary",)),
        name="inproj_rglru",
    )(x, row(g_pre), w_in, conv_w, row(conv_b), wa, row(b_rg_a), wx, row(b_rg_x), row(lru_lambda))


def _attn_kernel(q_ref, k_ref, v_ref, g_ref, xq_ref, xg_ref, mem_ref, gm_ref, wm_ref,
                 lq1_ref, lk1_ref, lq2_ref, lk2_ref, gs_ref,
                 yda_ref, yxa_ref, vt_ref, km_ref, vmt_ref, *, tq):
    h = pl.program_id(1)
    hd = DA_V_DIM
    dh = XA_HEAD_DIM
    s_len = q_ref.shape[0]
    n_mem = mem_ref.shape[0]
    nq = s_len // tq
    w = 2 * tq

    @pl.when(h == 0)
    def _():
        mn = _rms(mem_ref[...], gm_ref[...]).astype(BF16)
        kv = jnp.dot(mn, wm_ref[...].astype(BF16), preferred_element_type=F32)
        for hh in range(XA_HEADS):
            km_ref[hh] = kv[:, hh * dh:(hh + 1) * dh].astype(BF16)
            lo = XA_W + hh * dh
            vmt_ref[hh, 0:dh, :] = kv[:, lo:lo + dh].T.astype(BF16)
            vmt_ref[hh, dh:dh + BF16_ROWS, :] = jnp.ones((BF16_ROWS, n_mem), BF16)

    vt_ref[0:hd, :] = v_ref[...].astype(F32).T.astype(BF16)
    vt_ref[hd:hd + BF16_ROWS, :] = jnp.ones((BF16_ROWS, s_len), BF16)

    lam = (jnp.exp(jnp.sum(lq1_ref[...] * lk1_ref[...], axis=-1, keepdims=True))
           - jnp.exp(jnp.sum(lq2_ref[...] * lk2_ref[...], axis=-1, keepdims=True))
           + LAMBDA_INIT)
    lane = lax.broadcasted_iota(jnp.int32, (tq, hd), 1)
    key = lax.broadcasted_iota(jnp.int32, (tq, w), 0)
    qry = lax.broadcasted_iota(jnp.int32, (tq, w), 1) & (tq - 1)
    causal = key <= qry

    def softmax_t(s3, vt):
        n = s3.shape[0] * SUBLANES
        m = jnp.max(jnp.max(s3, axis=0), axis=0, keepdims=True)
        p = jnp.exp2(s3 - m[None]).reshape(n, s3.shape[2]).astype(BF16)
        return jnp.dot(vt, p, preferred_element_type=F32)

    def da_scores(qb):
        kv = (qb + 1) * tq
        q = q_ref[qb * tq:kv, :].astype(F32)
        qq = jnp.concatenate([jnp.where(lane < DA_HEAD_DIM, q, 0.0),
                              jnp.where(lane >= DA_HEAD_DIM, q, 0.0)], axis=0)
        qqt = qq.T.astype(BF16)
        s_diag = jnp.dot(k_ref[kv - tq:kv, :], qqt, preferred_element_type=F32)
        parts = [jnp.where(causal, s_diag, -jnp.inf)]
        if qb > 0:
            parts.insert(0, jnp.dot(k_ref[0:kv - tq, :], qqt, preferred_element_type=F32))
        return jnp.concatenate(parts, axis=0).reshape(kv // SUBLANES, SUBLANES, w)

    def da_finish(qb, s3):
        rows = slice(qb * tq, (qb + 1) * tq)
        pv = softmax_t(s3, vt_ref[:, 0:(qb + 1) * tq])
        ot = pv[0:hd] / pv[hd:hd + 1]
        od = (ot[:, 0:tq] - lam * ot[:, tq:w]).T
        o = _rms(od, gs_ref[...]) * (1.0 - LAMBDA_INIT)
        yda_ref[rows, :] = (o * _silu(g_ref[rows, :].astype(F32))).astype(BF16)

    def xa_scores(_):
        return _nt_dot(km_ref[h], xq_ref[...]).reshape(n_mem // SUBLANES, SUBLANES, s_len)

    def xa_finish(_, s3):
        pv = softmax_t(s3, vmt_ref[h])
        o = (pv[0:dh] / pv[dh:dh + 1]).T
        yxa_ref[...] = (o * _silu(xg_ref[...].astype(F32))).astype(BF16)

    chains = [(xa_scores, xa_finish, None)]
    chains += [(da_scores, da_finish, qb) for qb in range(nq - 1, -1, -1)]
    ahead = 2
    pending = [sc(arg) for sc, _, arg in chains[:ahead]]
    for pos, (_, finish, arg) in enumerate(chains):
        s3 = pending.pop(0)
        if pos + ahead < len(chains):
            sc, _, nxt = chains[pos + ahead]
            pending.append(sc(nxt))
        finish(arg, s3)


def _attn(proj, mem, g_mem, w_mem_kv, lq1, lk1, lq2, lk2, g_subln, tq):
    b, s, _ = proj.shape
    m, d = mem.shape[1:]
    hd = DA_V_DIM
    assert XA_HEAD_DIM == hd and XA_HEADS == DA_HEADS
    per_chunk = PROJ_CHUNK // hd
    vec = lambda v: v.reshape(1, -1)
    cvec = lambda n: pl.BlockSpec((1, n), lambda bi, h: (0, 0))
    col = lambda chunk: pl.BlockSpec((None, s, hd), lambda bi, h: (bi, 0, (chunk - N_RG_CHUNKS) * per_chunk + h))
    out = pl.BlockSpec((None, s, hd), lambda bi, h: (bi, 0, h))
    return pl.pallas_call(
        functools.partial(_attn_kernel, tq=tq),
        grid=(b, DA_HEADS),
        in_specs=[col(C_DA_Q), col(C_DA_K), col(C_DA_V), col(C_DA_G), col(C_XA_Q), col(C_XA_G),
                  pl.BlockSpec((None, m, d), lambda bi, h: (bi, 0, 0)),
                  cvec(d),
                  _resident((d, 2 * XA_W), lambda bi, h: (0, 0)),
                  cvec(DA_HEAD_DIM), cvec(DA_HEAD_DIM), cvec(DA_HEAD_DIM), cvec(DA_HEAD_DIM),
                  cvec(DA_V_DIM)],
        out_specs=[out, out],
        out_shape=[jax.ShapeDtypeStruct((b, s, DA_W), BF16), jax.ShapeDtypeStruct((b, s, XA_W), BF16)],
        scratch_shapes=[pltpu.VMEM((hd + BF16_ROWS, s), BF16),
                        pltpu.VMEM((XA_HEADS, m, hd), BF16),
                        pltpu.VMEM((XA_HEADS, hd + BF16_ROWS, m), BF16)],
        compiler_params=_params(("arbitrary", "arbitrary")),
        name="attn",
    )(proj, proj, proj, proj, proj, proj, mem, vec(g_mem), w_mem_kv,
      vec(lq1), vec(lk1), vec(lq2), vec(lk2), vec(g_subln))


def _outproj_kernel(yrg_ref, yda_ref, yxa_ref, w_ref, x_ref, g_ref, o_ref):
    w_rg = w_ref[0:RG_W, :].astype(BF16)
    w_da = w_ref[RG_W:RG_W + DA_W, :].astype(BF16)
    w_xa = w_ref[RG_W + DA_W:, :].astype(BF16)
    n_groups = 4
    rows_per = x_ref.shape[0] // n_groups
    for r in range(n_groups):
        rows = slice(r * rows_per, (r + 1) * rows_per)
        y = jnp.dot(yrg_ref[rows, :], w_rg, preferred_element_type=F32)
        y = y + jnp.dot(yda_ref[rows, :], w_da, preferred_element_type=F32)
        y = y + jnp.dot(yxa_ref[rows, :], w_xa, preferred_element_type=F32)
        o_ref[rows, :] = x_ref[rows, :] + _rms(y, g_ref[...])


def _outproj(y_rg, y_da, y_xa, w_out, x, g_post, tm):
    b, s, d = x.shape
    d_mix = w_out.shape[0]
    tile = lambda width: pl.BlockSpec((None, tm, width), lambda i, j: (i, j, 0))
    return pl.pallas_call(
        _outproj_kernel,
        grid=(b, s // tm),
        in_specs=[
            tile(RG_W), tile(DA_W), tile(XA_W),
            _resident((d_mix, d), lambda i, j: (0, 0)),
            tile(d),
            pl.BlockSpec((1, d), lambda i, j: (0, 0)),
        ],
        out_specs=tile(d),
        out_shape=jax.ShapeDtypeStruct((b, s, d), F32),
        compiler_params=_params(("arbitrary", "arbitrary")),
        name="outproj",
    )(y_rg, y_da, y_xa, w_out, x, g_post.reshape(1, d))


def kernel(x, mem, g_pre, g_mem, w_in, w_mem_kv, conv_w, conv_b, w_rg_a, b_rg_a, w_rg_x, b_rg_x,
           lru_lambda, lambda_q1, lambda_k1, lambda_q2, lambda_k2, g_subln, w_out, g_post):
    proj, y_rg = _inproj_rglru(x, g_pre, w_in, conv_w, conv_b, w_rg_a, b_rg_a, w_rg_x, b_rg_x,
                               lru_lambda, ts=128)
    y_da, y_xa = _attn(proj, mem, g_mem, w_mem_kv,
                       lambda_q1, lambda_k1, lambda_q2, lambda_k2, g_subln, tq=256)
    return _outproj(y_rg, y_da, y_xa, w_out, x, g_post, tm=1024)
```

```python
import functools
import math

import jax
import jax.numpy as jnp
from jax import lax
from jax.experimental import pallas as pl
from jax.experimental.pallas import tpu as pltpu

F32 = jnp.float32
BF16 = jnp.bfloat16

RMS_EPS = 1e-6
N_MEM = 256
RG_HEADS = 8
RG_HEAD_DIM = 64
RG_W = RG_HEADS * RG_HEAD_DIM
CONV_WIDTH = 4
LRU_C = 8.0
DA_HEADS = 4
DA_HEAD_DIM = 64
DA_V_DIM = 2 * DA_HEAD_DIM
DA_W = DA_HEADS * DA_V_DIM
LAMBDA_INIT = 0.8 - 0.6 * math.exp(-0.3 * (1 - 1))
XA_HEADS = 4
XA_HEAD_DIM = 128
XA_W = XA_HEADS * XA_HEAD_DIM
LOG2E = 1.4426950408889634

PROJ_CHUNK = 512
(C_RG_X, C_RG_G, C_DA_Q, C_DA_K, C_DA_V, C_DA_G, C_XA_Q, C_XA_G) = range(8)
D_IN = 8 * PROJ_CHUNK

SUBLANES = 8
LANES = 128
BF16_ROWS = 16
MXU_DIM = 256
VMEM_LIMIT_BYTES = 56 * 1024 * 1024


def _resident(shape, index_map):
    return pl.BlockSpec(shape, index_map, pipeline_mode=pl.Buffered(1))


def _params(sem):
    return pltpu.CompilerParams(dimension_semantics=sem, vmem_limit_bytes=VMEM_LIMIT_BYTES)


def _rms(x, g):
    ms = jnp.mean(x * x, axis=-1, keepdims=True)
    return x * lax.rsqrt(ms + RMS_EPS) * g


def _silu(x):
    h = 0.5 * x
    return h * (1.0 + jnp.tanh(h))


def _nt_dot(a, b):
    return lax.dot_general(a, b, (((1,), (1,)), ((), ())), preferred_element_type=F32)


RG_SUB = MXU_DIM // SUBLANES
N_RG_CHUNKS = 2


def _block_diag(w, group):
    h, di, dj = w.shape
    w = w.reshape(h // group, group, di, dj)
    eye = jnp.eye(group, dtype=w.dtype)
    return (eye[None, :, None, :, None] * w[:, :, :, None, :]).reshape(h // group, group * di, group * dj)


def _inproj_rglru_kernel(x_ref, g_ref, w_ref, cw_ref, cb_ref, wg_ref, ba_ref, bx_ref, lam_ref,
                         proj_ref, y_ref, rg_scr, xp_ref, a_ref, u_ref, h_ref, *, nt, nb, ts):
    j = pl.program_id(0)
    halo = CONV_WIDTH - 1
    sub = RG_SUB
    rows = sub * nb
    n_sub = ts // sub
    d = x_ref.shape[-1]
    assert rows == MXU_DIM and nb == SUBLANES and ts % sub == 0
    q_scale = {C_DA_Q: (DA_HEAD_DIM ** -0.5) * LOG2E, C_XA_Q: (XA_HEAD_DIM ** -0.5) * LOG2E}

    def normed():
        return _rms(x_ref[...].reshape(nb * ts, d), g_ref[...]).astype(BF16)

    def project(hn, chunks):
        for n in chunks:
            cols = slice(n * PROJ_CHUNK, (n + 1) * PROJ_CHUNK)
            acc = jnp.dot(hn, w_ref[:, cols].astype(BF16), preferred_element_type=F32)
            if n in q_scale:
                acc = acc * q_scale[n]
            acc = acc.astype(BF16).reshape(nb, ts, PROJ_CHUNK)
            if n < N_RG_CHUNKS:
                rg_scr[j & 1, :, :, cols] = acc
            else:
                out_cols = slice((n - N_RG_CHUNKS) * PROJ_CHUNK, (n - N_RG_CHUNKS + 1) * PROJ_CHUNK)
                proj_ref[:, :, out_cols] = acc

    def permutations():
        r = lax.broadcasted_iota(jnp.int32, (rows, rows), 0)
        c = lax.broadcasted_iota(jnp.int32, (rows, rows), 1)
        log_nb, log_sub = nb.bit_length() - 1, sub.bit_length() - 1
        to_time_major = (c == (r & (nb - 1)) * sub + (r >> log_nb)).astype(BF16)
        to_batch_major = (c == (r & (sub - 1)) * nb + (r >> log_sub)).astype(BF16)
        return to_time_major, to_batch_major

    def rg_load(to_time_major):
        xp_ref[0:halo] = xp_ref[ts:ts + halo]
        prev = (j + 1) & 1
        for sb in range(n_sub):
            t0 = sb * sub
            blk = rg_scr[prev, :, t0:t0 + sub, 0:RG_W].reshape(rows, RG_W)
            tmaj = jnp.dot(to_time_major, blk, preferred_element_type=F32)
            xp_ref[halo + t0:halo + t0 + sub] = tmaj.reshape(sub, nb, RG_W)

    def rg_recur():
        xh = 0.5 * cb_ref[...].reshape(1, 1, RG_W)
        for k in range(CONV_WIDTH):
            xh = xh + (0.5 * cw_ref[k:k + 1, :]).reshape(1, 1, RG_W) * xp_ref[k:k + ts]
        xb = xh.reshape(ts * nb, RG_W).astype(BF16)

        n_blk = RG_W // MXU_DIM

        def gate(which, bg_ref):
            z = [jnp.dot(xb[:, m * MXU_DIM:(m + 1) * MXU_DIM], wg_ref[which * n_blk + m],
                         preferred_element_type=F32) for m in range(n_blk)]
            return jnp.tanh(jnp.concatenate(z, axis=1) + 0.5 * bg_ref[...]).reshape(ts, nb, RG_W)

        tr = gate(0, ba_ref)
        ti = gate(1, bx_ref)
        z = -lam_ref[...]
        softplus = jnp.maximum(z, 0.0) + jnp.log1p(jnp.exp(-jnp.abs(z)))
        half_rate = ((-0.5 * LRU_C * LOG2E) * softplus).reshape(1, 1, RG_W)
        a = jnp.exp2(tr * half_rate + half_rate)
        d = jnp.maximum(1.0 - a * a, 0.0)
        mult = jnp.where(d > 0.0, d * lax.rsqrt(d), 0.0)
        a_ref[...] = a
        u_ref[...] = mult * (xh * (1.0 + ti))
        h = h_ref[...]
        for t in range(ts):
            h = a_ref[t] * h + u_ref[t]
            u_ref[t] = h
        h_ref[...] = h

    def rg_store(to_batch_major):
        prev = (j + 1) & 1
        for sb in range(n_sub):
            t0 = sb * sub
            h_tm = u_ref[t0:t0 + sub].reshape(rows, RG_W).astype(BF16)
            h_bm = jnp.dot(to_batch_major, h_tm, preferred_element_type=F32).reshape(nb, sub, RG_W)
            g_bm = rg_scr[prev, :, t0:t0 + sub, RG_W:2 * RG_W].astype(F32)
            y_ref[:, t0:t0 + sub, :] = (h_bm * _silu(g_bm)).astype(BF16)

    n_chunks = D_IN // PROJ_CHUNK

    @pl.when(j == 0)
    def _():
        xp_ref[...] = jnp.zeros(xp_ref.shape, F32)
        h_ref[...] = jnp.zeros(h_ref.shape, F32)
        project(normed(), range(n_chunks))

    @pl.when(jnp.logical_and(j > 0, j < nt))
    def _():
        to_tm, to_bm = permutations()
        hn = normed()
        rg_load(to_tm)
        project(hn, range(0, n_chunks // 2))
        rg_recur()
        project(hn, range(n_chunks // 2, n_chunks))
        rg_store(to_bm)

    @pl.when(j == nt)
    def _():
        to_tm, to_bm = permutations()
        rg_load(to_tm)
        rg_recur()
        rg_store(to_bm)


def _inproj_rglru(x, g_pre, w_in, conv_w, conv_b, w_rg_a, b_rg_a, w_rg_x, b_rg_x, lru_lambda, ts):
    nb, s, d = x.shape
    nt = s // ts
    group = MXU_DIM // RG_HEAD_DIM
    w_gates = _block_diag(jnp.concatenate([w_rg_a, w_rg_x], axis=0), group).astype(BF16)
    row = lambda v: v.reshape(1, -1)
    const2 = lambda shape: pl.BlockSpec(shape, lambda j: (0, 0))
    const3 = lambda shape: pl.BlockSpec(shape, lambda j: (0, 0, 0))
    proj_idx = lambda j: (0, jnp.minimum(j, nt - 1), 0)
    y_idx = lambda j: (0, jnp.maximum(j - 1, 0), 0)
    w_out = D_IN - N_RG_CHUNKS * PROJ_CHUNK
    return pl.pallas_call(
        functools.partial(_inproj_rglru_kernel, nt=nt, nb=nb, ts=ts),
        grid=(nt + 1,),
        in_specs=[pl.BlockSpec((nb, ts, d), proj_idx),
                  const2((1, d)),
                  _resident((d, D_IN), lambda j: (0, 0)),
                  const2((CONV_WIDTH, RG_W)), const2((1, RG_W)),
                  const3(w_gates.shape), const2((1, RG_W)), const2((1, RG_W)),
                  const2((1, RG_W))],
        out_specs=[pl.BlockSpec((nb, ts, w_out), proj_idx),
                   pl.BlockSpec((nb, ts, RG_W), y_idx)],
        out_shape=[jax.ShapeDtypeStruct((nb, s, w_out), BF16),
                   jax.ShapeDtypeStruct((nb, s, RG_W), BF16)],
        scratch_shapes=[
            pltpu.VMEM((2, nb, ts, N_RG_CHUNKS * PROJ_CHUNK), BF16),
            pltpu.VMEM((ts + CONV_WIDTH - 1, nb, RG_W), F32),
            pltpu.VMEM((ts, nb, RG_W), F32),
            pltpu.VMEM((ts, nb, RG_W), F32),
            pltpu.VMEM((nb, RG_W), F32),
        ],
        compiler_params=_params(("arbitrary",)),
        name="inproj_rglru",
    )(x, row(g_pre), w_in, conv_w, row(conv_b), w_gates, row(b_rg_a), row(b_rg_x), row(lru_lambda))


def _attn_kernel(q_ref, k_ref, v_ref, g_ref, xq_ref, xg_ref, mem_ref, gm_ref, wm_ref,
                 lq1_ref, lk1_ref, lq2_ref, lk2_ref, gs_ref,
                 yda_ref, yxa_ref, vt_ref, km_ref, vmt_ref, *, tq):
    h = pl.program_id(1)
    hd = DA_V_DIM
    dh = XA_HEAD_DIM
    s_len = q_ref.shape[0]
    n_mem = mem_ref.shape[0]
    nq = s_len // tq
    w = 2 * tq

    @pl.when(h == 0)
    def _():
        mn = _rms(mem_ref[...], gm_ref[...]).astype(BF16)
        kv = jnp.dot(mn, wm_ref[...].astype(BF16), preferred_element_type=F32)
        for hh in range(XA_HEADS):
            km_ref[hh] = kv[:, hh * dh:(hh + 1) * dh].astype(BF16)
            lo = XA_W + hh * dh
            vmt_ref[hh, 0:dh, :] = kv[:, lo:lo + dh].T.astype(BF16)
            vmt_ref[hh, dh:dh + BF16_ROWS, :] = jnp.ones((BF16_ROWS, n_mem), BF16)

    vt_ref[0:hd, :] = v_ref[...].astype(F32).T.astype(BF16)
    vt_ref[hd:hd + BF16_ROWS, :] = jnp.ones((BF16_ROWS, s_len), BF16)

    lam = (jnp.exp(jnp.sum(lq1_ref[...] * lk1_ref[...], axis=-1, keepdims=True))
           - jnp.exp(jnp.sum(lq2_ref[...] * lk2_ref[...], axis=-1, keepdims=True))
           + LAMBDA_INIT)
    lane = lax.broadcasted_iota(jnp.int32, (tq, hd), 1)
    key = lax.broadcasted_iota(jnp.int32, (tq, w), 0)
    qry = lax.broadcasted_iota(jnp.int32, (tq, w), 1) & (tq - 1)
    causal = key <= qry

    def softmax_t(s3, vt):
        n = s3.shape[0] * SUBLANES
        m = jnp.max(jnp.max(s3, axis=0), axis=0, keepdims=True)
        p = jnp.exp2(s3 - m[None]).reshape(n, s3.shape[2]).astype(BF16)
        return jnp.dot(vt, p, preferred_element_type=F32)

    def da_scores(qb):
        kv = (qb + 1) * tq
        q = q_ref[qb * tq:kv, :].astype(F32)
        qq = jnp.concatenate([jnp.where(lane < DA_HEAD_DIM, q, 0.0),
                              jnp.where(lane >= DA_HEAD_DIM, q, 0.0)], axis=0)
        qqt = qq.T.astype(BF16)
        s_diag = jnp.dot(k_ref[kv - tq:kv, :], qqt, preferred_element_type=F32)
        parts = [jnp.where(causal, s_diag, -jnp.inf)]
        if qb > 0:
            parts.insert(0, jnp.dot(k_ref[0:kv - tq, :], qqt, preferred_element_type=F32))
        return jnp.concatenate(parts, axis=0).reshape(kv // SUBLANES, SUBLANES, w)

    def da_finish(qb, s3):
        rows = slice(qb * tq, (qb + 1) * tq)
        pv = softmax_t(s3, vt_ref[:, 0:(qb + 1) * tq])
        ot = pv[0:hd] / pv[hd:hd + 1]
        od = (ot[:, 0:tq] - lam * ot[:, tq:w]).T
        o = _rms(od, gs_ref[...]) * (1.0 - LAMBDA_INIT)
        yda_ref[rows, :] = (o * _silu(g_ref[rows, :].astype(F32))).astype(BF16)

    def xa_scores(_):
        return _nt_dot(km_ref[h], xq_ref[...]).reshape(n_mem // SUBLANES, SUBLANES, s_len)

    def xa_finish(_, s3):
        pv = softmax_t(s3, vmt_ref[h])
        o = (pv[0:dh] / pv[dh:dh + 1]).T
        yxa_ref[...] = (o * _silu(xg_ref[...].astype(F32))).astype(BF16)

    chains = [(xa_scores, xa_finish, None)]
    chains += [(da_scores, da_finish, qb) for qb in range(nq - 1, -1, -1)]
    ahead = 2
    pending = [sc(arg) for sc, _, arg in chains[:ahead]]
    for pos, (_, finish, arg) in enumerate(chains):
        s3 = pending.pop(0)
        if pos + ahead < len(chains):
            sc, _, nxt = chains[pos + ahead]
            pending.append(sc(nxt))
        finish(arg, s3)


def _attn(proj, mem, g_mem, w_mem_kv, lq1, lk1, lq2, lk2, g_subln, tq):
    b, s, _ = proj.shape
    m, d = mem.shape[1:]
    hd = DA_V_DIM
    assert XA_HEAD_DIM == hd and XA_HEADS == DA_HEADS
    per_chunk = PROJ_CHUNK // hd
    vec = lambda v: v.reshape(1, -1)
    cvec = lambda n: pl.BlockSpec((1, n), lambda bi, h: (0, 0))
    col = lambda chunk: pl.BlockSpec((None, s, hd), lambda bi, h: (bi, 0, (chunk - N_RG_CHUNKS) * per_chunk + h))
    out = pl.BlockSpec((None, s, hd), lambda bi, h: (bi, 0, h))
    return pl.pallas_call(
        functools.partial(_attn_kernel, tq=tq),
        grid=(b, DA_HEADS),
        in_specs=[col(C_DA_Q), col(C_DA_K), col(C_DA_V), col(C_DA_G), col(C_XA_Q), col(C_XA_G),
                  pl.BlockSpec((None, m, d), lambda bi, h: (bi, 0, 0)),
                  cvec(d),
                  _resident((d, 2 * XA_W), lambda bi, h: (0, 0)),
                  cvec(DA_HEAD_DIM), cvec(DA_HEAD_DIM), cvec(DA_HEAD_DIM), cvec(DA_HEAD_DIM),
                  cvec(DA_V_DIM)],
        out_specs=[out, out],
        out_shape=[jax.ShapeDtypeStruct((b, s, DA_W), BF16), jax.ShapeDtypeStruct((b, s, XA_W), BF16)],
        scratch_shapes=[pltpu.VMEM((hd + BF16_ROWS, s), BF16),
                        pltpu.VMEM((XA_HEADS, m, hd), BF16),
                        pltpu.VMEM((XA_HEADS, hd + BF16_ROWS, m), BF16)],
        compiler_params=_params(("arbitrary", "arbitrary")),
        name="attn",
    )(proj, proj, proj, proj, proj, proj, mem, vec(g_mem), w_mem_kv,
      vec(lq1), vec(lk1), vec(lq2), vec(lk2), vec(g_subln))


def _outproj_kernel(yrg_ref, yda_ref, yxa_ref, w_ref, x_ref, g_ref, o_ref):
    w_rg = w_ref[0:RG_W, :].astype(BF16)
    w_da = w_ref[RG_W:RG_W + DA_W, :].astype(BF16)
    w_xa = w_ref[RG_W + DA_W:, :].astype(BF16)
    n_groups = 4
    rows_per = x_ref.shape[0] // n_groups
    for r in range(n_groups):
        rows = slice(r * rows_per, (r + 1) * rows_per)
        y = jnp.dot(yrg_ref[rows, :], w_rg, preferred_element_type=F32)
        y = y + jnp.dot(yda_ref[rows, :], w_da, preferred_element_type=F32)
        y = y + jnp.dot(yxa_ref[rows, :], w_xa, preferred_element_type=F32)
        o_ref[rows, :] = x_ref[rows, :] + _rms(y, g_ref[...])


def _outproj(y_rg, y_da, y_xa, w_out, x, g_post, tm):
    b, s, d = x.shape
    d_mix = w_out.shape[0]
    tile = lambda width: pl.BlockSpec((None, tm, width), lambda i, j: (i, j, 0))
    return pl.pallas_call(
        _outproj_kernel,
        grid=(b, s // tm),
        in_specs=[
            tile(RG_W), tile(DA_W), tile(XA_W),
            _resident((d_mix, d), lambda i, j: (0, 0)),
            tile(d),
            pl.BlockSpec((1, d), lambda i, j: (0, 0)),
        ],
        out_specs=tile(d),
        out_shape=jax.ShapeDtypeStruct((b, s, d), F32),
        compiler_params=_params(("arbitrary", "arbitrary")),
        name="outproj",
    )(y_rg, y_da, y_xa, w_out, x, g_post.reshape(1, d))


def kernel(x, mem, g_pre, g_mem, w_in, w_mem_kv, conv_w, conv_b, w_rg_a, b_rg_a, w_rg_x, b_rg_x,
           lru_lambda, lambda_q1, lambda_k1, lambda_q2, lambda_k2, g_subln, w_out, g_post):
    proj, y_rg = _inproj_rglru(x, g_pre, w_in, conv_w, conv_b, w_rg_a, b_rg_a, w_rg_x, b_rg_x,
                               lru_lambda, ts=128)
    y_da, y_xa = _attn(proj, mem, g_mem, w_mem_kv,
                       lambda_q1, lambda_k1, lambda_q2, lambda_k2, g_subln, tq=256)
    return _outproj(y_rg, y_da, y_xa, w_out, x, g_post, tm=1024)
```

```python
import functools
import math

import jax
import jax.numpy as jnp
from jax import lax
from jax.experimental import pallas as pl
from jax.experimental.pallas import tpu as pltpu

F32 = jnp.float32
BF16 = jnp.bfloat16

RMS_EPS = 1e-6
N_MEM = 256
RG_HEADS = 8
RG_HEAD_DIM = 64
RG_W = RG_HEADS * RG_HEAD_DIM
CONV_WIDTH = 4
LRU_C = 8.0
DA_HEADS = 4
DA_HEAD_DIM = 64
DA_V_DIM = 2 * DA_HEAD_DIM
DA_W = DA_HEADS * DA_V_DIM
LAMBDA_INIT = 0.8 - 0.6 * math.exp(-0.3 * (1 - 1))
XA_HEADS = 4
XA_HEAD_DIM = 128
XA_W = XA_HEADS * XA_HEAD_DIM
LOG2E = 1.4426950408889634

PROJ_CHUNK = 512
(C_RG_X, C_RG_G, C_DA_Q, C_DA_K, C_DA_V, C_DA_G, C_XA_Q, C_XA_G) = range(8)
D_IN = 8 * PROJ_CHUNK

SUBLANES = 8
LANES = 128
BF16_ROWS = 16
MXU_DIM = 256
VMEM_LIMIT_BYTES = 56 * 1024 * 1024


def _resident(shape, index_map):
    return pl.BlockSpec(shape, index_map, pipeline_mode=pl.Buffered(1))


def _params(sem):
    return pltpu.CompilerParams(dimension_semantics=sem, vmem_limit_bytes=VMEM_LIMIT_BYTES)


def _rms(x, g):
    ms = jnp.mean(x * x, axis=-1, keepdims=True)
    return x * lax.rsqrt(ms + RMS_EPS) * g


def _silu(x):
    h = 0.5 * x
    return h * (1.0 + jnp.tanh(h))


def _nt_dot(a, b):
    return lax.dot_general(a, b, (((1,), (1,)), ((), ())), preferred_element_type=F32)


RG_SUB = MXU_DIM // SUBLANES
N_RG_CHUNKS = 2


def _block_diag(w, group):
    h, di, dj = w.shape
    w = w.reshape(h // group, group, di, dj)
    eye = jnp.eye(group, dtype=w.dtype)
    return (eye[None, :, None, :, None] * w[:, :, :, None, :]).reshape(h // group, group * di, group * dj)


def _inproj_rglru_kernel(x_ref, g_ref, w_ref, cw_ref, cb_ref, wg_ref, ba_ref, bx_ref, lam_ref,
                         proj_ref, y_ref, rg_scr, xp_ref, a_ref, u_ref, h_ref, *, nt, nb, ts):
    j = pl.program_id(0)
    halo = CONV_WIDTH - 1
    sub = RG_SUB
    rows = sub * nb
    n_sub = ts // sub
    d = x_ref.shape[-1]
    assert rows == MXU_DIM and nb == SUBLANES and ts % sub == 0
    q_scale = {C_DA_Q: (DA_HEAD_DIM ** -0.5) * LOG2E, C_XA_Q: (XA_HEAD_DIM ** -0.5) * LOG2E}

    def normed():
        return _rms(x_ref[...].reshape(nb * ts, d), g_ref[...]).astype(BF16)

    def project(hn, chunks):
        for n in chunks:
            cols = slice(n * PROJ_CHUNK, (n + 1) * PROJ_CHUNK)
            acc = jnp.dot(hn, w_ref[:, cols].astype(BF16), preferred_element_type=F32)
            if n in q_scale:
                acc = acc * q_scale[n]
            acc = acc.astype(BF16).reshape(nb, ts, PROJ_CHUNK)
            if n < N_RG_CHUNKS:
                rg_scr[j & 1, :, :, cols] = acc
            else:
                out_cols = slice((n - N_RG_CHUNKS) * PROJ_CHUNK, (n - N_RG_CHUNKS + 1) * PROJ_CHUNK)
                proj_ref[:, :, out_cols] = acc

    def permutations():
        r = lax.broadcasted_iota(jnp.int32, (rows, rows), 0)
        c = lax.broadcasted_iota(jnp.int32, (rows, rows), 1)
        log_nb, log_sub = nb.bit_length() - 1, sub.bit_length() - 1
        to_time_major = (c == (r & (nb - 1)) * sub + (r >> log_nb)).astype(BF16)
        to_batch_major = (c == (r & (sub - 1)) * nb + (r >> log_sub)).astype(BF16)
        return to_time_major, to_batch_major

    def rg_load(to_time_major):
        xp_ref[0:halo] = xp_ref[ts:ts + halo]
        prev = (j + 1) & 1
        for sb in range(n_sub):
            t0 = sb * sub
            blk = rg_scr[prev, :, t0:t0 + sub, 0:RG_W].reshape(rows, RG_W)
            tmaj = jnp.dot(to_time_major, blk, preferred_element_type=F32)
            xp_ref[halo + t0:halo + t0 + sub] = tmaj.reshape(sub, nb, RG_W)

    def rg_recur():
        xh = 0.5 * cb_ref[...].reshape(1, 1, RG_W)
        for k in range(CONV_WIDTH):
            xh = xh + (0.5 * cw_ref[k:k + 1, :]).reshape(1, 1, RG_W) * xp_ref[k:k + ts]
        xb = xh.reshape(ts * nb, RG_W).astype(BF16)

        n_blk = RG_W // MXU_DIM

        def gate(which, bg_ref):
            z = [jnp.dot(xb[:, m * MXU_DIM:(m + 1) * MXU_DIM], wg_ref[which * n_blk + m],
                         preferred_element_type=F32) for m in range(n_blk)]
            return jnp.tanh(jnp.concatenate(z, axis=1) + 0.5 * bg_ref[...]).reshape(ts, nb, RG_W)

        tr = gate(0, ba_ref)
        ti = gate(1, bx_ref)
        z = -lam_ref[...]
        softplus = jnp.maximum(z, 0.0) + jnp.log1p(jnp.exp(-jnp.abs(z)))
        half_rate = ((-0.5 * LRU_C * LOG2E) * softplus).reshape(1, 1, RG_W)
        a = jnp.exp2(tr * half_rate + half_rate)
        d = jnp.maximum(1.0 - a * a, 0.0)
        mult = jnp.where(d > 0.0, d * lax.rsqrt(d), 0.0)
        a_ref[...] = a
        u_ref[...] = mult * (xh * (1.0 + ti))
        h = h_ref[...]
        for t in range(ts):
            h = a_ref[t] * h + u_ref[t]
            u_ref[t] = h
        h_ref[...] = h

    def rg_store(to_batch_major):
        prev = (j + 1) & 1
        for sb in range(n_sub):
            t0 = sb * sub
            h_tm = u_ref[t0:t0 + sub].reshape(rows, RG_W).astype(BF16)
            h_bm = jnp.dot(to_batch_major, h_tm, preferred_element_type=F32).reshape(nb, sub, RG_W)
            g_bm = rg_scr[prev, :, t0:t0 + sub, RG_W:2 * RG_W].astype(F32)
            y_ref[:, t0:t0 + sub, :] = (h_bm * _silu(g_bm)).astype(BF16)

    n_chunks = D_IN // PROJ_CHUNK

    @pl.when(j == 0)
    def _():
        xp_ref[...] = jnp.zeros(xp_ref.shape, F32)
        h_ref[...] = jnp.zeros(h_ref.shape, F32)
        project(normed(), range(n_chunks))

    @pl.when(jnp.logical_and(j > 0, j < nt))
    def _():
        to_tm, to_bm = permutations()
        hn = normed()
        rg_load(to_tm)
        project(hn, range(0, n_chunks // 2))
        rg_recur()
        project(hn, range(n_chunks // 2, n_chunks))
        rg_store(to_bm)

    @pl.when(j == nt)
    def _():
        to_tm, to_bm = permutations()
        rg_load(to_tm)
        rg_recur()
        rg_store(to_bm)


def _inproj_rglru(x, g_pre, w_in, conv_w, conv_b, w_rg_a, b_rg_a, w_rg_x, b_rg_x, lru_lambda, ts):
    nb, s, d = x.shape
    nt = s // ts
    group = MXU_DIM // RG_HEAD_DIM
    w_gates = _block_diag(jnp.concatenate([w_rg_a, w_rg_x], axis=0), group).astype(BF16)
    row = lambda v: v.reshape(1, -1)
    const2 = lambda shape: pl.BlockSpec(shape, lambda j: (0, 0))
    const3 = lambda shape: pl.BlockSpec(shape, lambda j: (0, 0, 0))
    proj_idx = lambda j: (0, jnp.minimum(j, nt - 1), 0)
    y_idx = lambda j: (0, jnp.maximum(j - 1, 0), 0)
    w_out = D_IN - N_RG_CHUNKS * PROJ_CHUNK
    return pl.pallas_call(
        functools.partial(_inproj_rglru_kernel, nt=nt, nb=nb, ts=ts),
        grid=(nt + 1,),
        in_specs=[pl.BlockSpec((nb, ts, d), proj_idx),
                  const2((1, d)),
                  _resident((d, D_IN), lambda j: (0, 0)),
                  const2((CONV_WIDTH, RG_W)), const2((1, RG_W)),
                  const3(w_gates.shape), const2((1, RG_W)), const2((1, RG_W)),
                  const2((1, RG_W))],
        out_specs=[pl.BlockSpec((nb, ts, w_out), proj_idx),
                   pl.BlockSpec((nb, ts, RG_W), y_idx)],
        out_shape=[jax.ShapeDtypeStruct((nb, s, w_out), BF16),
                   jax.ShapeDtypeStruct((nb, s, RG_W), BF16)],
        scratch_shapes=[
            pltpu.VMEM((2, nb, ts, N_RG_CHUNKS * PROJ_CHUNK), BF16),
            pltpu.VMEM((ts + CONV_WIDTH - 1, nb, RG_W), F32),
            pltpu.VMEM((ts, nb, RG_W), F32),
            pltpu.VMEM((ts, nb, RG_W), F32),
            pltpu.VMEM((nb, RG_W), F32),
        ],
        compiler_params=_params(("arbitrary",)),
        name="inproj_rglru",
    )(x, row(g_pre), w_in, conv_w, row(conv_b), w_gates, row(b_rg_a), row(b_rg_x), row(lru_lambda))


def _attn_out_kernel(q_ref, k_ref, v_ref, g_ref, xq_ref, xg_ref, mem_ref, gm_ref, wm_ref,
                     lq1_ref, lk1_ref, lq2_ref, lk2_ref, gs_ref, yrg_ref, x_ref, wo_ref, gp_ref,
                     out_ref, vt_ref, km_ref, vmt_ref, y_scr, *, tq, nb):
    b = pl.program_id(0)
    h = pl.program_id(1)
    hd = DA_V_DIM
    dh = XA_HEAD_DIM
    s_len = q_ref.shape[0]
    n_mem = mem_ref.shape[0]
    nq = s_len // tq
    w = 2 * tq
    rows_o = out_ref.shape[0]
    cur = b & 1
    prev = (b + 1) & 1

    @pl.when(jnp.logical_and(h == 0, b < nb))
    def _():
        mn = _rms(mem_ref[...], gm_ref[...]).astype(BF16)
        kv = jnp.dot(mn, wm_ref[...].astype(BF16), preferred_element_type=F32)
        for hh in range(XA_HEADS):
            km_ref[hh] = kv[:, hh * dh:(hh + 1) * dh].astype(BF16)
            lo = XA_W + hh * dh
            vmt_ref[hh, 0:dh, :] = kv[:, lo:lo + dh].T.astype(BF16)
            vmt_ref[hh, dh:dh + BF16_ROWS, :] = jnp.ones((BF16_ROWS, n_mem), BF16)

    def softmax_t(s3, vt):
        n = s3.shape[0] * SUBLANES
        m = jnp.max(jnp.max(s3, axis=0), axis=0, keepdims=True)
        p = jnp.exp2(s3 - m[None]).reshape(n, s3.shape[2]).astype(BF16)
        return jnp.dot(vt, p, preferred_element_type=F32)

    def attention_chains():
        vt_ref[0:hd, :] = v_ref[...].astype(F32).T.astype(BF16)
        vt_ref[hd:hd + BF16_ROWS, :] = jnp.ones((BF16_ROWS, s_len), BF16)
        lam = (jnp.exp(jnp.sum(lq1_ref[...] * lk1_ref[...], axis=-1, keepdims=True))
               - jnp.exp(jnp.sum(lq2_ref[...] * lk2_ref[...], axis=-1, keepdims=True))
               + LAMBDA_INIT)
        lane = lax.broadcasted_iota(jnp.int32, (tq, hd), 1)
        key = lax.broadcasted_iota(jnp.int32, (tq, w), 0)
        qry = lax.broadcasted_iota(jnp.int32, (tq, w), 1) & (tq - 1)
        causal = key <= qry

        def da_scores(qb):
            kv = (qb + 1) * tq
            q = q_ref[qb * tq:kv, :].astype(F32)
            qq = jnp.concatenate([jnp.where(lane < DA_HEAD_DIM, q, 0.0),
                                  jnp.where(lane >= DA_HEAD_DIM, q, 0.0)], axis=0)
            qqt = qq.T.astype(BF16)
            s_diag = jnp.dot(k_ref[kv - tq:kv, :], qqt, preferred_element_type=F32)
            parts = [jnp.where(causal, s_diag, -jnp.inf)]
            if qb > 0:
                parts.insert(0, jnp.dot(k_ref[0:kv - tq, :], qqt, preferred_element_type=F32))
            return jnp.concatenate(parts, axis=0).reshape(kv // SUBLANES, SUBLANES, w)

        def da_finish(qb, s3):
            rows = slice(qb * tq, (qb + 1) * tq)
            pv = softmax_t(s3, vt_ref[:, 0:(qb + 1) * tq])
            ot = pv[0:hd] / pv[hd:hd + 1]
            od = (ot[:, 0:tq] - lam * ot[:, tq:w]).T
            o = _rms(od, gs_ref[...]) * (1.0 - LAMBDA_INIT)
            y_scr[cur, h, rows, :] = (o * _silu(g_ref[rows, :].astype(F32))).astype(BF16)

        def xa_scores(_):
            return _nt_dot(km_ref[h], xq_ref[...]).reshape(n_mem // SUBLANES, SUBLANES, s_len)

        def xa_finish(_, s3):
            pv = softmax_t(s3, vmt_ref[h])
            o = (pv[0:dh] / pv[dh:dh + 1]).T
            y_scr[cur, DA_HEADS + h] = (o * _silu(xg_ref[...].astype(F32))).astype(BF16)

        return ([(xa_scores, xa_finish, None)]
                + [(da_scores, da_finish, qb) for qb in range(nq - 1, -1, -1)])

    def op_products(_):
        r0 = pl.multiple_of(h * rows_o, rows_o)
        y_att = jnp.concatenate([y_scr[prev, c, pl.ds(r0, rows_o), :] for c in range(DA_HEADS + XA_HEADS)],
                                axis=1)
        y = jnp.dot(yrg_ref[...], wo_ref[0:RG_W, :].astype(BF16), preferred_element_type=F32)
        return y + jnp.dot(y_att, wo_ref[RG_W:, :].astype(BF16), preferred_element_type=F32)

    def op_finish(_, y):
        out_ref[...] = x_ref[...] + _rms(y, gp_ref[...])

    def run(chains):
        ahead = 2
        pending = [first(arg) for first, _, arg in chains[:ahead]]
        for pos, (_, finish, arg) in enumerate(chains):
            carried = pending.pop(0)
            if pos + ahead < len(chains):
                first, _, nxt = chains[pos + ahead]
                pending.append(first(nxt))
            finish(arg, carried)

    op_chain = [(op_products, op_finish, None)]

    @pl.when(b == 0)
    def _():
        run(attention_chains())

    @pl.when(jnp.logical_and(b > 0, b < nb))
    def _():
        run(op_chain + attention_chains())

    @pl.when(b == nb)
    def _():
        run(op_chain)


def _attn_out(proj, y_rg, x, mem, g_mem, w_mem_kv, lq1, lk1, lq2, lk2, g_subln, w_out, g_post, tq):
    nb, s, d = x.shape
    m = mem.shape[1]
    hd = DA_V_DIM
    assert XA_HEAD_DIM == hd and XA_HEADS == DA_HEADS and s % DA_HEADS == 0
    per_chunk = PROJ_CHUNK // hd
    rows_o = s // DA_HEADS
    vec = lambda v: v.reshape(1, -1)
    cvec = lambda n: pl.BlockSpec((1, n), lambda bi, h: (0, 0))
    att_b = lambda bi: jnp.minimum(bi, nb - 1)
    att_h = lambda bi, h: jnp.where(bi < nb, h, DA_HEADS - 1)
    col = lambda chunk: pl.BlockSpec(
        (None, s, hd), lambda bi, h: (att_b(bi), 0, (chunk - N_RG_CHUNKS) * per_chunk + att_h(bi, h)))
    op_idx = lambda bi, h: (jnp.maximum(bi - 1, 0), jnp.where(bi > 0, h, 0), 0)
    return pl.pallas_call(
        functools.partial(_attn_out_kernel, tq=tq, nb=nb),
        grid=(nb + 1, DA_HEADS),
        in_specs=[col(C_DA_Q), col(C_DA_K), col(C_DA_V), col(C_DA_G), col(C_XA_Q), col(C_XA_G),
                  pl.BlockSpec((None, m, d), lambda bi, h: (att_b(bi), 0, 0)),
                  cvec(d),
                  _resident((d, 2 * XA_W), lambda bi, h: (0, 0)),
                  cvec(DA_HEAD_DIM), cvec(DA_HEAD_DIM), cvec(DA_HEAD_DIM), cvec(DA_HEAD_DIM),
                  cvec(DA_V_DIM),
                  pl.BlockSpec((None, rows_o, RG_W), op_idx),
                  pl.BlockSpec((None, rows_o, d), op_idx),
                  _resident(w_out.shape, lambda bi, h: (0, 0)),
                  cvec(d)],
        out_specs=pl.BlockSpec((None, rows_o, d), op_idx),
        out_shape=jax.ShapeDtypeStruct((nb, s, d), F32),
        scratch_shapes=[pltpu.VMEM((hd + BF16_ROWS, s), BF16),
                        pltpu.VMEM((XA_HEADS, m, hd), BF16),
                        pltpu.VMEM((XA_HEADS, hd + BF16_ROWS, m), BF16),
                        pltpu.VMEM((2, DA_HEADS + XA_HEADS, s, hd), BF16)],
        compiler_params=_params(("arbitrary", "arbitrary")),
        name="attn_out",
    )(proj, proj, proj, proj, proj, proj, mem, vec(g_mem), w_mem_kv,
      vec(lq1), vec(lk1), vec(lq2), vec(lk2), vec(g_subln), y_rg, x, w_out, vec(g_post))


def kernel(x, mem, g_pre, g_mem, w_in, w_mem_kv, conv_w, conv_b, w_rg_a, b_rg_a, w_rg_x, b_rg_x,
           lru_lambda, lambda_q1, lambda_k1, lambda_q2, lambda_k2, g_subln, w_out, g_post):
    proj, y_rg = _inproj_rglru(x, g_pre, w_in, conv_w, conv_b, w_rg_a, b_rg_a, w_rg_x, b_rg_x,
                               lru_lambda, ts=128)
    return _attn_out(proj, y_rg, x, mem, g_mem, w_mem_kv, lambda_q1, lambda_k1, lambda_q2, lambda_k2,
                     g_subln, w_out, g_post, tq=256)
```

```python
import functools
import math

import jax
import jax.numpy as jnp
from jax import lax
from jax.experimental import pallas as pl
from jax.experimental.pallas import tpu as pltpu

F32 = jnp.float32
BF16 = jnp.bfloat16

RMS_EPS = 1e-6
N_MEM = 256
RG_HEADS = 8
RG_HEAD_DIM = 64
RG_W = RG_HEADS * RG_HEAD_DIM
CONV_WIDTH = 4
LRU_C = 8.0
DA_HEADS = 4
DA_HEAD_DIM = 64
DA_V_DIM = 2 * DA_HEAD_DIM
DA_W = DA_HEADS * DA_V_DIM
LAMBDA_INIT = 0.8 - 0.6 * math.exp(-0.3 * (1 - 1))
XA_HEADS = 4
XA_HEAD_DIM = 128
XA_W = XA_HEADS * XA_HEAD_DIM
LOG2E = 1.4426950408889634

PROJ_CHUNK = 512
(C_RG_X, C_RG_G, C_DA_Q, C_DA_K, C_DA_V, C_DA_G, C_XA_Q, C_XA_G) = range(8)
D_IN = 8 * PROJ_CHUNK

SUBLANES = 8
LANES = 128
BF16_ROWS = 16
MXU_DIM = 256
VMEM_LIMIT_BYTES = 56 * 1024 * 1024


def _resident(shape, index_map):
    return pl.BlockSpec(shape, index_map, pipeline_mode=pl.Buffered(1))


def _params(sem):
    return pltpu.CompilerParams(dimension_semantics=sem, vmem_limit_bytes=VMEM_LIMIT_BYTES)


def _rms(x, g):
    ms = jnp.mean(x * x, axis=-1, keepdims=True)
    return x * lax.rsqrt(ms + RMS_EPS) * g


def _silu(x):
    h = 0.5 * x
    return h * (1.0 + jnp.tanh(h))


def _nt_dot(a, b):
    return lax.dot_general(a, b, (((1,), (1,)), ((), ())), preferred_element_type=F32)


RG_SUB = MXU_DIM // SUBLANES
N_RG_CHUNKS = 2


def _block_diag(w, group):
    h, di, dj = w.shape
    w = w.reshape(h // group, group, di, dj)
    eye = jnp.eye(group, dtype=w.dtype)
    return (eye[None, :, None, :, None] * w[:, :, :, None, :]).reshape(h // group, group * di, group * dj)


def _inproj_rglru_kernel(x_ref, g_ref, w_ref, cw_ref, cb_ref, wg_ref, ba_ref, bx_ref, lam_ref,
                         proj_ref, y_ref, rg_scr, xp_ref, a_ref, u_ref, h_ref, *, nt, nb, ts):
    j = pl.program_id(0)
    halo = CONV_WIDTH - 1
    sub = RG_SUB
    rows = sub * nb
    n_sub = ts // sub
    d = x_ref.shape[-1]
    assert rows == MXU_DIM and nb == SUBLANES and ts % sub == 0
    q_scale = {C_DA_Q: (DA_HEAD_DIM ** -0.5) * LOG2E, C_XA_Q: (XA_HEAD_DIM ** -0.5) * LOG2E}

    def normed():
        return _rms(x_ref[...].reshape(nb * ts, d), g_ref[...]).astype(BF16)

    def project(hn, chunks):
        for n in chunks:
            cols = slice(n * PROJ_CHUNK, (n + 1) * PROJ_CHUNK)
            acc = jnp.dot(hn, w_ref[:, cols].astype(BF16), preferred_element_type=F32)
            if n in q_scale:
                acc = acc * q_scale[n]
            acc = acc.astype(BF16).reshape(nb, ts, PROJ_CHUNK)
            if n < N_RG_CHUNKS:
                rg_scr[j & 1, :, :, cols] = acc
            else:
                out_cols = slice((n - N_RG_CHUNKS) * PROJ_CHUNK, (n - N_RG_CHUNKS + 1) * PROJ_CHUNK)
                proj_ref[:, :, out_cols] = acc

    def permutations():
        r = lax.broadcasted_iota(jnp.int32, (rows, rows), 0)
        c = lax.broadcasted_iota(jnp.int32, (rows, rows), 1)
        log_nb, log_sub = nb.bit_length() - 1, sub.bit_length() - 1
        to_time_major = (c == (r & (nb - 1)) * sub + (r >> log_nb)).astype(BF16)
        to_batch_major = (c == (r & (sub - 1)) * nb + (r >> log_sub)).astype(BF16)
        return to_time_major, to_batch_major

    def rg_load(to_time_major):
        xp_ref[0:halo] = xp_ref[ts:ts + halo]
        prev = (j + 1) & 1
        for sb in range(n_sub):
            t0 = sb * sub
            blk = rg_scr[prev, :, t0:t0 + sub, 0:RG_W].reshape(rows, RG_W)
            tmaj = jnp.dot(to_time_major, blk, preferred_element_type=F32)
            xp_ref[halo + t0:halo + t0 + sub] = tmaj.reshape(sub, nb, RG_W)

    def rg_recur():
        xh = 0.5 * cb_ref[...].reshape(1, 1, RG_W)
        for k in range(CONV_WIDTH):
            xh = xh + (0.5 * cw_ref[k:k + 1, :]).reshape(1, 1, RG_W) * xp_ref[k:k + ts]
        xb = xh.reshape(ts * nb, RG_W).astype(BF16)

        n_blk = RG_W // MXU_DIM

        def gate(which, bg_ref):
            z = [jnp.dot(xb[:, m * MXU_DIM:(m + 1) * MXU_DIM], wg_ref[which * n_blk + m],
                         preferred_element_type=F32) for m in range(n_blk)]
            return jnp.tanh(jnp.concatenate(z, axis=1) + 0.5 * bg_ref[...]).reshape(ts, nb, RG_W)

        tr = gate(0, ba_ref)
        ti = gate(1, bx_ref)
        z = -lam_ref[...]
        softplus = jnp.maximum(z, 0.0) + jnp.log1p(jnp.exp(-jnp.abs(z)))
        half_rate = ((-0.5 * LRU_C * LOG2E) * softplus).reshape(1, 1, RG_W)
        a = jnp.exp2(tr * half_rate + half_rate)
        d = jnp.maximum(1.0 - a * a, 0.0)
        mult = jnp.where(d > 0.0, d * lax.rsqrt(d), 0.0)
        a_ref[...] = a
        u_ref[...] = mult * (xh * (1.0 + ti))
        h = h_ref[...]
        for t in range(ts):
            h = a_ref[t] * h + u_ref[t]
            u_ref[t] = h
        h_ref[...] = h

    def rg_store(to_batch_major):
        prev = (j + 1) & 1
        for sb in range(n_sub):
            t0 = sb * sub
            h_tm = u_ref[t0:t0 + sub].reshape(rows, RG_W).astype(BF16)
            h_bm = jnp.dot(to_batch_major, h_tm, preferred_element_type=F32).reshape(nb, sub, RG_W)
            g_bm = rg_scr[prev, :, t0:t0 + sub, RG_W:2 * RG_W].astype(F32)
            y_ref[:, t0:t0 + sub, :] = (h_bm * _silu(g_bm)).astype(BF16)

    n_chunks = D_IN // PROJ_CHUNK

    @pl.when(j == 0)
    def _():
        xp_ref[...] = jnp.zeros(xp_ref.shape, F32)
        h_ref[...] = jnp.zeros(h_ref.shape, F32)
        project(normed(), range(n_chunks))

    @pl.when(jnp.logical_and(j > 0, j < nt))
    def _():
        to_tm, to_bm = permutations()
        hn = normed()
        rg_load(to_tm)
        project(hn, range(0, n_chunks // 2))
        rg_recur()
        project(hn, range(n_chunks // 2, n_chunks))
        rg_store(to_bm)

    @pl.when(j == nt)
    def _():
        to_tm, to_bm = permutations()
        rg_load(to_tm)
        rg_recur()
        rg_store(to_bm)


def _inproj_rglru(x, g_pre, w_in, conv_w, conv_b, w_rg_a, b_rg_a, w_rg_x, b_rg_x, lru_lambda, ts):
    nb, s, d = x.shape
    nt = s // ts
    group = MXU_DIM // RG_HEAD_DIM
    w_gates = _block_diag(jnp.concatenate([w_rg_a, w_rg_x], axis=0), group).astype(BF16)
    row = lambda v: v.reshape(1, -1)
    const2 = lambda shape: pl.BlockSpec(shape, lambda j: (0, 0))
    const3 = lambda shape: pl.BlockSpec(shape, lambda j: (0, 0, 0))
    proj_idx = lambda j: (0, jnp.minimum(j, nt - 1), 0)
    y_idx = lambda j: (0, jnp.maximum(j - 1, 0), 0)
    w_out = D_IN - N_RG_CHUNKS * PROJ_CHUNK
    return pl.pallas_call(
        functools.partial(_inproj_rglru_kernel, nt=nt, nb=nb, ts=ts),
        grid=(nt + 1,),
        in_specs=[pl.BlockSpec((nb, ts, d), proj_idx),
                  const2((1, d)),
                  _resident((d, D_IN), lambda j: (0, 0)),
                  const2((CONV_WIDTH, RG_W)), const2((1, RG_W)),
                  const3(w_gates.shape), const2((1, RG_W)), const2((1, RG_W)),
                  const2((1, RG_W))],
        out_specs=[pl.BlockSpec((nb, ts, w_out), proj_idx),
                   pl.BlockSpec((nb, ts, RG_W), y_idx)],
        out_shape=[jax.ShapeDtypeStruct((nb, s, w_out), BF16),
                   jax.ShapeDtypeStruct((nb, s, RG_W), BF16)],
        scratch_shapes=[
            pltpu.VMEM((2, nb, ts, N_RG_CHUNKS * PROJ_CHUNK), BF16),
            pltpu.VMEM((ts + CONV_WIDTH - 1, nb, RG_W), F32),
            pltpu.VMEM((ts, nb, RG_W), F32),
            pltpu.VMEM((ts, nb, RG_W), F32),
            pltpu.VMEM((nb, RG_W), F32),
        ],
        compiler_params=_params(("arbitrary",)),
        name="inproj_rglru",
    )(x, row(g_pre), w_in, conv_w, row(conv_b), w_gates, row(b_rg_a), row(b_rg_x), row(lru_lambda))


def _attn_out_kernel(q_ref, k_ref, v_ref, g_ref, xq_ref, xg_ref, mem_ref, gm_ref, wm_ref,
                     lq1_ref, lk1_ref, lq2_ref, lk2_ref, gs_ref, yrg_ref, x_ref, wo_ref, gp_ref,
                     out_ref, vt_ref, km_ref, vmt_ref, y_scr, *, tq, nb):
    b = pl.program_id(0)
    h = pl.program_id(1)
    hd = DA_V_DIM
    dh = XA_HEAD_DIM
    s_len = q_ref.shape[0]
    n_mem = mem_ref.shape[0]
    nq = s_len // tq
    w = 2 * tq
    rows_o = out_ref.shape[0]
    cur = b & 1
    prev = (b + 1) & 1

    @pl.when(jnp.logical_and(h == 0, b < nb))
    def _():
        mn = _rms(mem_ref[...], gm_ref[...]).astype(BF16)
        kv = jnp.dot(mn, wm_ref[...].astype(BF16), preferred_element_type=F32)
        for hh in range(XA_HEADS):
            km_ref[hh] = kv[:, hh * dh:(hh + 1) * dh].astype(BF16)
            lo = XA_W + hh * dh
            vmt_ref[hh, 0:dh, :] = kv[:, lo:lo + dh].T.astype(BF16)
            vmt_ref[hh, dh:dh + BF16_ROWS, :] = jnp.ones((BF16_ROWS, n_mem), BF16)

    def softmax_t(s3, vt):
        n = s3.shape[0] * SUBLANES
        m = jnp.max(jnp.max(s3, axis=0), axis=0, keepdims=True)
        p = jnp.exp2(s3 - m[None]).reshape(n, s3.shape[2]).astype(BF16)
        return jnp.dot(vt, p, preferred_element_type=F32)

    def attention_chains():
        vt_ref[0:hd, :] = v_ref[...].astype(F32).T.astype(BF16)
        vt_ref[hd:hd + BF16_ROWS, :] = jnp.ones((BF16_ROWS, s_len), BF16)
        lam = (jnp.exp(jnp.sum(lq1_ref[...] * lk1_ref[...], axis=-1, keepdims=True))
               - jnp.exp(jnp.sum(lq2_ref[...] * lk2_ref[...], axis=-1, keepdims=True))
               + LAMBDA_INIT)
        lane = lax.broadcasted_iota(jnp.int32, (tq, hd), 1)
        key = lax.broadcasted_iota(jnp.int32, (tq, w), 0)
        qry = lax.broadcasted_iota(jnp.int32, (tq, w), 1) & (tq - 1)
        causal = key <= qry

        def da_scores(qb):
            kv = (qb + 1) * tq
            q = q_ref[qb * tq:kv, :].astype(F32)
            qq = jnp.concatenate([jnp.where(lane < DA_HEAD_DIM, q, 0.0),
                                  jnp.where(lane >= DA_HEAD_DIM, q, 0.0)], axis=0)
            qqt = qq.T.astype(BF16)
            s_diag = jnp.dot(k_ref[kv - tq:kv, :], qqt, preferred_element_type=F32)
            parts = [jnp.where(causal, s_diag, -jnp.inf)]
            if qb > 0:
                parts.insert(0, jnp.dot(k_ref[0:kv - tq, :], qqt, preferred_element_type=F32))
            return jnp.concatenate(parts, axis=0).reshape(kv // SUBLANES, SUBLANES, w)

        def da_finish(qb, s3):
            rows = slice(qb * tq, (qb + 1) * tq)
            pv = softmax_t(s3, vt_ref[:, 0:(qb + 1) * tq])
            ot = pv[0:hd] / pv[hd:hd + 1]
            od = (ot[:, 0:tq] - lam * ot[:, tq:w]).T
            o = _rms(od, gs_ref[...]) * (1.0 - LAMBDA_INIT)
            y_scr[cur, h, rows, :] = (o * _silu(g_ref[rows, :].astype(F32))).astype(BF16)

        def xa_scores(_):
            return _nt_dot(km_ref[h], xq_ref[...]).reshape(n_mem // SUBLANES, SUBLANES, s_len)

        def xa_finish(_, s3):
            pv = softmax_t(s3, vmt_ref[h])
            o = (pv[0:dh] / pv[dh:dh + 1]).T
            y_scr[cur, DA_HEADS + h] = (o * _silu(xg_ref[...].astype(F32))).astype(BF16)

        return ([(xa_scores, xa_finish, None)]
                + [(da_scores, da_finish, qb) for qb in range(nq - 1, -1, -1)])

    def op_products(_):
        r0 = pl.multiple_of(h * rows_o, rows_o)
        y_att = jnp.concatenate([y_scr[prev, c, pl.ds(r0, rows_o), :] for c in range(DA_HEADS + XA_HEADS)],
                                axis=1)
        y = jnp.dot(yrg_ref[...], wo_ref[0:RG_W, :].astype(BF16), preferred_element_type=F32)
        return y + jnp.dot(y_att, wo_ref[RG_W:, :].astype(BF16), preferred_element_type=F32)

    def op_finish(_, y):
        out_ref[...] = x_ref[...] + _rms(y, gp_ref[...])

    def run(chains):
        ahead = 2
        pending = [first(arg) for first, _, arg in chains[:ahead]]
        for pos, (_, finish, arg) in enumerate(chains):
            carried = pending.pop(0)
            if pos + ahead < len(chains):
                first, _, nxt = chains[pos + ahead]
                pending.append(first(nxt))
            finish(arg, carried)

    op_chain = [(op_products, op_finish, None)]

    @pl.when(b == 0)
    def _():
        run(attention_chains())

    @pl.when(jnp.logical_and(b > 0, b < nb))
    def _():
        att = attention_chains()
        run(att[:1] + op_chain + att[1:])

    @pl.when(b == nb)
    def _():
        run(op_chain)


def _attn_out(proj, y_rg, x, mem, g_mem, w_mem_kv, lq1, lk1, lq2, lk2, g_subln, w_out, g_post, tq):
    nb, s, d = x.shape
    m = mem.shape[1]
    hd = DA_V_DIM
    assert XA_HEAD_DIM == hd and XA_HEADS == DA_HEADS and s % DA_HEADS == 0
    per_chunk = PROJ_CHUNK // hd
    rows_o = s // DA_HEADS
    vec = lambda v: v.reshape(1, -1)
    cvec = lambda n: pl.BlockSpec((1, n), lambda bi, h: (0, 0))
    att_b = lambda bi: jnp.minimum(bi, nb - 1)
    att_h = lambda bi, h: jnp.where(bi < nb, h, DA_HEADS - 1)
    col = lambda chunk: pl.BlockSpec(
        (None, s, hd), lambda bi, h: (att_b(bi), 0, (chunk - N_RG_CHUNKS) * per_chunk + att_h(bi, h)))
    op_idx = lambda bi, h: (jnp.maximum(bi - 1, 0), jnp.where(bi > 0, h, 0), 0)
    return pl.pallas_call(
        functools.partial(_attn_out_kernel, tq=tq, nb=nb),
        grid=(nb + 1, DA_HEADS),
        in_specs=[col(C_DA_Q), col(C_DA_K), col(C_DA_V), col(C_DA_G), col(C_XA_Q), col(C_XA_G),
                  pl.BlockSpec((None, m, d), lambda bi, h: (att_b(bi), 0, 0)),
                  cvec(d),
                  _resident((d, 2 * XA_W), lambda bi, h: (0, 0)),
                  cvec(DA_HEAD_DIM), cvec(DA_HEAD_DIM), cvec(DA_HEAD_DIM), cvec(DA_HEAD_DIM),
                  cvec(DA_V_DIM),
                  pl.BlockSpec((None, rows_o, RG_W), op_idx),
                  pl.BlockSpec((None, rows_o, d), op_idx),
                  _resident(w_out.shape, lambda bi, h: (0, 0)),
                  cvec(d)],
        out_specs=pl.BlockSpec((None, rows_o, d), op_idx),
        out_shape=jax.ShapeDtypeStruct((nb, s, d), F32),
        scratch_shapes=[pltpu.VMEM((hd + BF16_ROWS, s), BF16),
                        pltpu.VMEM((XA_HEADS, m, hd), BF16),
                        pltpu.VMEM((XA_HEADS, hd + BF16_ROWS, m), BF16),
                        pltpu.VMEM((2, DA_HEADS + XA_HEADS, s, hd), BF16)],
        compiler_params=_params(("arbitrary", "arbitrary")),
        name="attn_out",
    )(proj, proj, proj, proj, proj, proj, mem, vec(g_mem), w_mem_kv,
      vec(lq1), vec(lk1), vec(lq2), vec(lk2), vec(g_subln), y_rg, x, w_out, vec(g_post))


def kernel(x, mem, g_pre, g_mem, w_in, w_mem_kv, conv_w, conv_b, w_rg_a, b_rg_a, w_rg_x, b_rg_x,
           lru_lambda, lambda_q1, lambda_k1, lambda_q2, lambda_k2, g_subln, w_out, g_post):
    proj, y_rg = _inproj_rglru(x, g_pre, w_in, conv_w, conv_b, w_rg_a, b_rg_a, w_rg_x, b_rg_x,
                               lru_lambda, ts=128)
    return _attn_out(proj, y_rg, x, mem, g_mem, w_mem_kv, lambda_q1, lambda_k1, lambda_q2, lambda_k2,
                     g_subln, w_out, g_post, tq=256)
```

```python
import functools
import math

import jax
import jax.numpy as jnp
from jax import lax
from jax.experimental import pallas as pl
from jax.experimental.pallas import tpu as pltpu

F32 = jnp.float32
BF16 = jnp.bfloat16

RMS_EPS = 1e-6
N_MEM = 256
RG_HEADS = 8
RG_HEAD_DIM = 64
RG_W = RG_HEADS * RG_HEAD_DIM
CONV_WIDTH = 4
LRU_C = 8.0
DA_HEADS = 4
DA_HEAD_DIM = 64
DA_V_DIM = 2 * DA_HEAD_DIM
DA_W = DA_HEADS * DA_V_DIM
LAMBDA_INIT = 0.8 - 0.6 * math.exp(-0.3 * (1 - 1))
XA_HEADS = 4
XA_HEAD_DIM = 128
XA_W = XA_HEADS * XA_HEAD_DIM
LOG2E = 1.4426950408889634

PROJ_CHUNK = 512
(C_RG_X, C_RG_G, C_DA_Q, C_DA_K, C_DA_V, C_DA_G, C_XA_Q, C_XA_G) = range(8)
D_IN = 8 * PROJ_CHUNK

SUBLANES = 8
LANES = 128
BF16_ROWS = 16
MXU_DIM = 256
VMEM_LIMIT_BYTES = 56 * 1024 * 1024


def _resident(shape, index_map):
    return pl.BlockSpec(shape, index_map, pipeline_mode=pl.Buffered(1))


def _params(sem):
    return pltpu.CompilerParams(dimension_semantics=sem, vmem_limit_bytes=VMEM_LIMIT_BYTES)


def _rms(x, g):
    ms = jnp.mean(x * x, axis=-1, keepdims=True)
    return x * lax.rsqrt(ms + RMS_EPS) * g


def _silu(x):
    h = 0.5 * x
    return h * (1.0 + jnp.tanh(h))


def _nt_dot(a, b):
    return lax.dot_general(a, b, (((1,), (1,)), ((), ())), preferred_element_type=F32)


RG_SUB = MXU_DIM // SUBLANES
N_RG_CHUNKS = 2


def _block_diag(w, group):
    h, di, dj = w.shape
    w = w.reshape(h // group, group, di, dj)
    eye = jnp.eye(group, dtype=w.dtype)
    return (eye[None, :, None, :, None] * w[:, :, :, None, :]).reshape(h // group, group * di, group * dj)


def _inproj_rglru_kernel(x_ref, g_ref, w_ref, cw_ref, cb_ref, wg_ref, ba_ref, bx_ref, lam_ref,
                         proj_ref, y_ref, rg_scr, xp_ref, a_ref, u_ref, h_ref, *, nt, nb, ts):
    j = pl.program_id(0)
    halo = CONV_WIDTH - 1
    sub = RG_SUB
    rows = sub * nb
    n_sub = ts // sub
    d = x_ref.shape[-1]
    assert rows == MXU_DIM and nb == SUBLANES and ts % sub == 0
    q_scale = {C_DA_Q: (DA_HEAD_DIM ** -0.5) * LOG2E, C_XA_Q: (XA_HEAD_DIM ** -0.5) * LOG2E}

    def normed():
        return _rms(x_ref[...].reshape(nb * ts, d), g_ref[...]).astype(BF16)

    def project(hn, chunks):
        for n in chunks:
            cols = slice(n * PROJ_CHUNK, (n + 1) * PROJ_CHUNK)
            acc = jnp.dot(hn, w_ref[:, cols].astype(BF16), preferred_element_type=F32)
            if n in q_scale:
                acc = acc * q_scale[n]
            acc = acc.astype(BF16).reshape(nb, ts, PROJ_CHUNK)
            if n < N_RG_CHUNKS:
                rg_scr[j & 1, :, :, cols] = acc
            else:
                out_cols = slice((n - N_RG_CHUNKS) * PROJ_CHUNK, (n - N_RG_CHUNKS + 1) * PROJ_CHUNK)
                proj_ref[:, :, out_cols] = acc

    def permutations():
        r = lax.broadcasted_iota(jnp.int32, (rows, rows), 0)
        c = lax.broadcasted_iota(jnp.int32, (rows, rows), 1)
        log_nb, log_sub = nb.bit_length() - 1, sub.bit_length() - 1
        to_time_major = (c == (r & (nb - 1)) * sub + (r >> log_nb)).astype(BF16)
        to_batch_major = (c == (r & (sub - 1)) * nb + (r >> log_sub)).astype(BF16)
        return to_time_major, to_batch_major

    def rg_load(to_time_major):
        xp_ref[0:halo] = xp_ref[ts:ts + halo]
        prev = (j + 1) & 1
        for sb in range(n_sub):
            t0 = sb * sub
            blk = rg_scr[prev, :, t0:t0 + sub, 0:RG_W].reshape(rows, RG_W)
            tmaj = jnp.dot(to_time_major, blk, preferred_element_type=F32)
            xp_ref[halo + t0:halo + t0 + sub] = tmaj.reshape(sub, nb, RG_W)

    def rg_recur():
        xh = 0.5 * cb_ref[...].reshape(1, 1, RG_W)
        for k in range(CONV_WIDTH):
            xh = xh + (0.5 * cw_ref[k:k + 1, :]).reshape(1, 1, RG_W) * xp_ref[k:k + ts]
        xb = xh.reshape(ts * nb, RG_W).astype(BF16)

        n_blk = RG_W // MXU_DIM

        def gate(which, bg_ref):
            z = [jnp.dot(xb[:, m * MXU_DIM:(m + 1) * MXU_DIM], wg_ref[which * n_blk + m],
                         preferred_element_type=F32) for m in range(n_blk)]
            return jnp.tanh(jnp.concatenate(z, axis=1) + 0.5 * bg_ref[...]).reshape(ts, nb, RG_W)

        tr = gate(0, ba_ref)
        ti = gate(1, bx_ref)
        z = -lam_ref[...]
        softplus = jnp.maximum(z, 0.0) + jnp.log1p(jnp.exp(-jnp.abs(z)))
        half_rate = ((-0.5 * LRU_C * LOG2E) * softplus).reshape(1, 1, RG_W)
        a = jnp.exp2(tr * half_rate + half_rate)
        d = jnp.maximum(1.0 - a * a, 0.0)
        mult = jnp.where(d > 0.0, d * lax.rsqrt(d), 0.0)
        a_ref[...] = a
        u_ref[...] = mult * (xh * (1.0 + ti))
        h = h_ref[...]
        for t in range(ts):
            h = a_ref[t] * h + u_ref[t]
            u_ref[t] = h
        h_ref[...] = h

    def rg_store(to_batch_major):
        prev = (j + 1) & 1
        for sb in range(n_sub):
            t0 = sb * sub
            h_tm = u_ref[t0:t0 + sub].reshape(rows, RG_W).astype(BF16)
            h_bm = jnp.dot(to_batch_major, h_tm, preferred_element_type=F32).reshape(nb, sub, RG_W)
            g_bm = rg_scr[prev, :, t0:t0 + sub, RG_W:2 * RG_W].astype(F32)
            y_ref[:, t0:t0 + sub, :] = (h_bm * _silu(g_bm)).astype(BF16)

    n_chunks = D_IN // PROJ_CHUNK

    @pl.when(j == 0)
    def _():
        xp_ref[...] = jnp.zeros(xp_ref.shape, F32)
        h_ref[...] = jnp.zeros(h_ref.shape, F32)
        project(normed(), range(n_chunks))

    @pl.when(jnp.logical_and(j > 0, j < nt))
    def _():
        to_tm, to_bm = permutations()
        hn = normed()
        rg_load(to_tm)
        project(hn, range(0, n_chunks // 2))
        rg_recur()
        project(hn, range(n_chunks // 2, n_chunks))
        rg_store(to_bm)

    @pl.when(j == nt)
    def _():
        to_tm, to_bm = permutations()
        rg_load(to_tm)
        rg_recur()
        rg_store(to_bm)


def _inproj_rglru(x, g_pre, w_in, conv_w, conv_b, w_rg_a, b_rg_a, w_rg_x, b_rg_x, lru_lambda, ts):
    nb, s, d = x.shape
    nt = s // ts
    group = MXU_DIM // RG_HEAD_DIM
    w_gates = _block_diag(jnp.concatenate([w_rg_a, w_rg_x], axis=0), group).astype(BF16)
    row = lambda v: v.reshape(1, -1)
    const2 = lambda shape: pl.BlockSpec(shape, lambda j: (0, 0))
    const3 = lambda shape: pl.BlockSpec(shape, lambda j: (0, 0, 0))
    proj_idx = lambda j: (0, jnp.minimum(j, nt - 1), 0)
    y_idx = lambda j: (0, jnp.maximum(j - 1, 0), 0)
    w_out = D_IN - N_RG_CHUNKS * PROJ_CHUNK
    return pl.pallas_call(
        functools.partial(_inproj_rglru_kernel, nt=nt, nb=nb, ts=ts),
        grid=(nt + 1,),
        in_specs=[pl.BlockSpec((nb, ts, d), proj_idx),
                  const2((1, d)),
                  _resident((d, D_IN), lambda j: (0, 0)),
                  const2((CONV_WIDTH, RG_W)), const2((1, RG_W)),
                  const3(w_gates.shape), const2((1, RG_W)), const2((1, RG_W)),
                  const2((1, RG_W))],
        out_specs=[pl.BlockSpec((nb, ts, w_out), proj_idx),
                   pl.BlockSpec((nb, ts, RG_W), y_idx)],
        out_shape=[jax.ShapeDtypeStruct((nb, s, w_out), BF16),
                   jax.ShapeDtypeStruct((nb, s, RG_W), BF16)],
        scratch_shapes=[
            pltpu.VMEM((2, nb, ts, N_RG_CHUNKS * PROJ_CHUNK), BF16),
            pltpu.VMEM((ts + CONV_WIDTH - 1, nb, RG_W), F32),
            pltpu.VMEM((ts, nb, RG_W), F32),
            pltpu.VMEM((ts, nb, RG_W), F32),
            pltpu.VMEM((nb, RG_W), F32),
        ],
        compiler_params=_params(("arbitrary",)),
        name="inproj_rglru",
    )(x, row(g_pre), w_in, conv_w, row(conv_b), w_gates, row(b_rg_a), row(b_rg_x), row(lru_lambda))


def _attn_out_kernel(q_ref, k_ref, v_ref, g_ref, xq_ref, xg_ref, mem_ref, gm_ref, wm_ref,
                     lq1_ref, lk1_ref, lq2_ref, lk2_ref, gs_ref, yrg_ref, x_ref, wo_ref, gp_ref,
                     out_ref, vt_ref, km_ref, vmt_ref, y_scr, *, tq, nb):
    b = pl.program_id(0)
    h = pl.program_id(1)
    hd = DA_V_DIM
    dh = XA_HEAD_DIM
    s_len = q_ref.shape[0]
    n_mem = mem_ref.shape[0]
    nq = s_len // tq
    w = 2 * tq
    rows_o = out_ref.shape[0]
    cur = b & 1
    prev = (b + 1) & 1

    @pl.when(jnp.logical_and(h == 0, b < nb))
    def _():
        mn = _rms(mem_ref[...], gm_ref[...]).astype(BF16)
        kv = jnp.dot(mn, wm_ref[...].astype(BF16), preferred_element_type=F32)
        for hh in range(XA_HEADS):
            km_ref[hh] = kv[:, hh * dh:(hh + 1) * dh].astype(BF16)
            lo = XA_W + hh * dh
            vmt_ref[hh, 0:dh, :] = kv[:, lo:lo + dh].T.astype(BF16)
            vmt_ref[hh, dh:dh + BF16_ROWS, :] = jnp.ones((BF16_ROWS, n_mem), BF16)

    def softmax_t(s3, vt):
        n = s3.shape[0] * SUBLANES
        m = jnp.max(jnp.max(s3, axis=0), axis=0, keepdims=True)
        p = jnp.exp2(s3 - m[None]).reshape(n, s3.shape[2]).astype(BF16)
        return jnp.dot(vt, p, preferred_element_type=F32)

    def attention_chains():
        vt_ref[0:hd, :] = v_ref[...].astype(F32).T.astype(BF16)
        vt_ref[hd:hd + BF16_ROWS, :] = jnp.ones((BF16_ROWS, s_len), BF16)
        lam = (jnp.exp(jnp.sum(lq1_ref[...] * lk1_ref[...], axis=-1, keepdims=True))
               - jnp.exp(jnp.sum(lq2_ref[...] * lk2_ref[...], axis=-1, keepdims=True))
               + LAMBDA_INIT)
        lane = lax.broadcasted_iota(jnp.int32, (tq, hd), 1)
        key = lax.broadcasted_iota(jnp.int32, (tq, w), 0)
        qry = lax.broadcasted_iota(jnp.int32, (tq, w), 1) & (tq - 1)
        causal = key <= qry

        def da_scores(qb):
            kv = (qb + 1) * tq
            q = q_ref[qb * tq:kv, :].astype(F32)
            qq = jnp.concatenate([jnp.where(lane < DA_HEAD_DIM, q, 0.0),
                                  jnp.where(lane >= DA_HEAD_DIM, q, 0.0)], axis=0)
            qqt = qq.T.astype(BF16)
            s_diag = jnp.dot(k_ref[kv - tq:kv, :], qqt, preferred_element_type=F32)
            parts = [jnp.where(causal, s_diag, -jnp.inf)]
            if qb > 0:
                parts.insert(0, jnp.dot(k_ref[0:kv - tq, :], qqt, preferred_element_type=F32))
            return jnp.concatenate(parts, axis=0).reshape(kv // SUBLANES, SUBLANES, w)

        def da_finish(qb, s3):
            rows = slice(qb * tq, (qb + 1) * tq)
            pv = softmax_t(s3, vt_ref[:, 0:(qb + 1) * tq])
            ot = pv[0:hd] / pv[hd:hd + 1]
            od = (ot[:, 0:tq] - lam * ot[:, tq:w]).T
            o = _rms(od, gs_ref[...]) * (1.0 - LAMBDA_INIT)
            y_scr[cur, h, rows, :] = (o * _silu(g_ref[rows, :].astype(F32))).astype(BF16)

        def xa_scores(_):
            return _nt_dot(km_ref[h], xq_ref[...]).reshape(n_mem // SUBLANES, SUBLANES, s_len)

        def xa_finish(_, s3):
            pv = softmax_t(s3, vmt_ref[h])
            o = (pv[0:dh] / pv[dh:dh + 1]).T
            y_scr[cur, DA_HEADS + h] = (o * _silu(xg_ref[...].astype(F32))).astype(BF16)

        return ([(xa_scores, xa_finish, None)]
                + [(da_scores, da_finish, qb) for qb in range(nq - 1, -1, -1)])

    def op_products(_):
        r0 = pl.multiple_of(h * rows_o, rows_o)
        y_att = jnp.concatenate([y_scr[prev, c, pl.ds(r0, rows_o), :] for c in range(DA_HEADS + XA_HEADS)],
                                axis=1)
        y = jnp.dot(yrg_ref[...], wo_ref[0:RG_W, :].astype(BF16), preferred_element_type=F32)
        return y + jnp.dot(y_att, wo_ref[RG_W:, :].astype(BF16), preferred_element_type=F32)

    def op_finish(_, y):
        out_ref[...] = x_ref[...] + _rms(y, gp_ref[...])

    def run(chains):
        ahead = 2
        pending = [first(arg) for first, _, arg in chains[:ahead]]
        for pos, (_, finish, arg) in enumerate(chains):
            carried = pending.pop(0)
            if pos + ahead < len(chains):
                first, _, nxt = chains[pos + ahead]
                pending.append(first(nxt))
            finish(arg, carried)

    op_chain = [(op_products, op_finish, None)]

    @pl.when(b == 0)
    def _():
        run(attention_chains())

    @pl.when(jnp.logical_and(b > 0, b < nb))
    def _():
        att = attention_chains()
        run(att[:2] + op_chain + att[2:])

    @pl.when(b == nb)
    def _():
        run(op_chain)


def _attn_out(proj, y_rg, x, mem, g_mem, w_mem_kv, lq1, lk1, lq2, lk2, g_subln, w_out, g_post, tq):
    nb, s, d = x.shape
    m = mem.shape[1]
    hd = DA_V_DIM
    assert XA_HEAD_DIM == hd and XA_HEADS == DA_HEADS and s % DA_HEADS == 0
    per_chunk = PROJ_CHUNK // hd
    rows_o = s // DA_HEADS
    vec = lambda v: v.reshape(1, -1)
    cvec = lambda n: pl.BlockSpec((1, n), lambda bi, h: (0, 0))
    att_b = lambda bi: jnp.minimum(bi, nb - 1)
    att_h = lambda bi, h: jnp.where(bi < nb, h, DA_HEADS - 1)
    col = lambda chunk: pl.BlockSpec(
        (None, s, hd), lambda bi, h: (att_b(bi), 0, (chunk - N_RG_CHUNKS) * per_chunk + att_h(bi, h)))
    op_idx = lambda bi, h: (jnp.maximum(bi - 1, 0), jnp.where(bi > 0, h, 0), 0)
    return pl.pallas_call(
        functools.partial(_attn_out_kernel, tq=tq, nb=nb),
        grid=(nb + 1, DA_HEADS),
        in_specs=[col(C_DA_Q), col(C_DA_K), col(C_DA_V), col(C_DA_G), col(C_XA_Q), col(C_XA_G),
                  pl.BlockSpec((None, m, d), lambda bi, h: (att_b(bi), 0, 0)),
                  cvec(d),
                  _resident((d, 2 * XA_W), lambda bi, h: (0, 0)),
                  cvec(DA_HEAD_DIM), cvec(DA_HEAD_DIM), cvec(DA_HEAD_DIM), cvec(DA_HEAD_DIM),
                  cvec(DA_V_DIM),
                  pl.BlockSpec((None, rows_o, RG_W), op_idx),
                  pl.BlockSpec((None, rows_o, d), op_idx),
                  _resident(w_out.shape, lambda bi, h: (0, 0)),
                  cvec(d)],
        out_specs=pl.BlockSpec((None, rows_o, d), op_idx),
        out_shape=jax.ShapeDtypeStruct((nb, s, d), F32),
        scratch_shapes=[pltpu.VMEM((hd + BF16_ROWS, s), BF16),
                        pltpu.VMEM((XA_HEADS, m, hd), BF16),
                        pltpu.VMEM((XA_HEADS, hd + BF16_ROWS, m), BF16),
                        pltpu.VMEM((2, DA_HEADS + XA_HEADS, s, hd), BF16)],
        compiler_params=_params(("arbitrary", "arbitrary")),
        name="attn_out",
    )(proj, proj, proj, proj, proj, proj, mem, vec(g_mem), w_mem_kv,
      vec(lq1), vec(lk1), vec(lq2), vec(lk2), vec(g_subln), y_rg, x, w_out, vec(g_post))


def kernel(x, mem, g_pre, g_mem, w_in, w_mem_kv, conv_w, conv_b, w_rg_a, b_rg_a, w_rg_x, b_rg_x,
           lru_lambda, lambda_q1, lambda_k1, lambda_q2, lambda_k2, g_subln, w_out, g_post):
    proj, y_rg = _inproj_rglru(x, g_pre, w_in, conv_w, conv_b, w_rg_a, b_rg_a, w_rg_x, b_rg_x,
                               lru_lambda, ts=128)
    return _attn_out(proj, y_rg, x, mem, g_mem, w_mem_kv, lambda_q1, lambda_k1, lambda_q2, lambda_k2,
                     g_subln, w_out, g_post, tq=256)
```

```python
import functools
import math

import jax
import jax.numpy as jnp
from jax import lax
from jax.experimental import pallas as pl
from jax.experimental.pallas import tpu as pltpu

F32 = jnp.float32
BF16 = jnp.bfloat16

RMS_EPS = 1e-6
N_MEM = 256
RG_HEADS = 8
RG_HEAD_DIM = 64
RG_W = RG_HEADS * RG_HEAD_DIM
CONV_WIDTH = 4
LRU_C = 8.0
DA_HEADS = 4
DA_HEAD_DIM = 64
DA_V_DIM = 2 * DA_HEAD_DIM
DA_W = DA_HEADS * DA_V_DIM
LAMBDA_INIT = 0.8 - 0.6 * math.exp(-0.3 * (1 - 1))
XA_HEADS = 4
XA_HEAD_DIM = 128
XA_W = XA_HEADS * XA_HEAD_DIM
LOG2E = 1.4426950408889634

PROJ_CHUNK = 512
(C_RG_X, C_RG_G, C_DA_Q, C_DA_K, C_DA_V, C_DA_G, C_XA_Q, C_XA_G) = range(8)
D_IN = 8 * PROJ_CHUNK

SUBLANES = 8
LANES = 128
BF16_ROWS = 16
MXU_DIM = 256
VMEM_LIMIT_BYTES = 56 * 1024 * 1024


def _resident(shape, index_map):
    return pl.BlockSpec(shape, index_map, pipeline_mode=pl.Buffered(1))


def _params(sem):
    return pltpu.CompilerParams(dimension_semantics=sem, vmem_limit_bytes=VMEM_LIMIT_BYTES)


def _rms(x, g):
    ms = jnp.mean(x * x, axis=-1, keepdims=True)
    return x * lax.rsqrt(ms + RMS_EPS) * g


def _silu(x):
    h = 0.5 * x
    return h * (1.0 + jnp.tanh(h))


def _nt_dot(a, b):
    return lax.dot_general(a, b, (((1,), (1,)), ((), ())), preferred_element_type=F32)


RG_SUB = MXU_DIM // SUBLANES
N_RG_CHUNKS = 2


def _inproj_rglru_kernel(x_ref, g_ref, w_ref, cw_ref, cb_ref, wa_ref, ba_ref, wx_ref, bx_ref, lam_ref,
                         proj_ref, y_ref, rg_scr, xp_ref, a_ref, u_ref, h_ref, wg_scr, *, nt, nb, ts):
    j = pl.program_id(0)
    halo = CONV_WIDTH - 1
    sub = RG_SUB
    rows = sub * nb
    n_sub = ts // sub
    d = x_ref.shape[-1]
    assert rows == MXU_DIM and nb == SUBLANES and ts % sub == 0
    q_scale = {C_DA_Q: (DA_HEAD_DIM ** -0.5) * LOG2E, C_XA_Q: (XA_HEAD_DIM ** -0.5) * LOG2E}

    def normed():
        return _rms(x_ref[...].reshape(nb * ts, d), g_ref[...]).astype(BF16)

    def project(hn, chunks):
        for n in chunks:
            cols = slice(n * PROJ_CHUNK, (n + 1) * PROJ_CHUNK)
            acc = jnp.dot(hn, w_ref[:, cols].astype(BF16), preferred_element_type=F32)
            if n in q_scale:
                acc = acc * q_scale[n]
            acc = acc.astype(BF16).reshape(nb, ts, PROJ_CHUNK)
            if n < N_RG_CHUNKS:
                rg_scr[j & 1, :, :, cols] = acc
            else:
                out_cols = slice((n - N_RG_CHUNKS) * PROJ_CHUNK, (n - N_RG_CHUNKS + 1) * PROJ_CHUNK)
                proj_ref[:, :, out_cols] = acc

    def permutations():
        r = lax.broadcasted_iota(jnp.int32, (rows, rows), 0)
        c = lax.broadcasted_iota(jnp.int32, (rows, rows), 1)
        log_nb, log_sub = nb.bit_length() - 1, sub.bit_length() - 1
        to_time_major = (c == (r & (nb - 1)) * sub + (r >> log_nb)).astype(BF16)
        to_batch_major = (c == (r & (sub - 1)) * nb + (r >> log_sub)).astype(BF16)
        return to_time_major, to_batch_major

    def rg_load(to_time_major):
        xp_ref[0:halo] = xp_ref[ts:ts + halo]
        prev = (j + 1) & 1
        for sb in range(n_sub):
            t0 = sb * sub
            blk = rg_scr[prev, :, t0:t0 + sub, 0:RG_W].reshape(rows, RG_W)
            tmaj = jnp.dot(to_time_major, blk, preferred_element_type=F32)
            xp_ref[halo + t0:halo + t0 + sub] = tmaj.reshape(sub, nb, RG_W)

    def rg_recur():
        xh = 0.5 * cb_ref[...].reshape(1, 1, RG_W)
        for k in range(CONV_WIDTH):
            xh = xh + (0.5 * cw_ref[k:k + 1, :]).reshape(1, 1, RG_W) * xp_ref[k:k + ts]
        xb = xh.reshape(ts * nb, RG_W).astype(BF16)

        n_blk = RG_W // MXU_DIM

        def gate(which, bg_ref):
            z = [jnp.dot(xb[:, m * MXU_DIM:(m + 1) * MXU_DIM], wg_scr[which * n_blk + m],
                         preferred_element_type=F32) for m in range(n_blk)]
            return jnp.tanh(jnp.concatenate(z, axis=1) + 0.5 * bg_ref[...]).reshape(ts, nb, RG_W)

        tr = gate(0, ba_ref)
        ti = gate(1, bx_ref)
        z = -lam_ref[...]
        softplus = jnp.maximum(z, 0.0) + jnp.log1p(jnp.exp(-jnp.abs(z)))
        half_rate = ((-0.5 * LRU_C * LOG2E) * softplus).reshape(1, 1, RG_W)
        a = jnp.exp2(tr * half_rate + half_rate)
        d = jnp.maximum(1.0 - a * a, 0.0)
        mult = jnp.where(d > 0.0, d * lax.rsqrt(d), 0.0)
        a_ref[...] = a
        u_ref[...] = mult * (xh * (1.0 + ti))
        h = h_ref[...]
        for t in range(ts):
            h = a_ref[t] * h + u_ref[t]
            u_ref[t] = h
        h_ref[...] = h

    def rg_store(to_batch_major):
        prev = (j + 1) & 1
        for sb in range(n_sub):
            t0 = sb * sub
            h_tm = u_ref[t0:t0 + sub].reshape(rows, RG_W).astype(BF16)
            h_bm = jnp.dot(to_batch_major, h_tm, preferred_element_type=F32).reshape(nb, sub, RG_W)
            g_bm = rg_scr[prev, :, t0:t0 + sub, RG_W:2 * RG_W].astype(F32)
            y_ref[:, t0:t0 + sub, :] = (h_bm * _silu(g_bm)).astype(BF16)

    n_chunks = D_IN // PROJ_CHUNK

    @pl.when(j == 0)
    def _():
        xp_ref[...] = jnp.zeros(xp_ref.shape, F32)
        h_ref[...] = jnp.zeros(h_ref.shape, F32)
        group = MXU_DIM // RG_HEAD_DIM
        lane_head = lax.broadcasted_iota(jnp.int32, (RG_HEAD_DIM, MXU_DIM), 1) // RG_HEAD_DIM
        for gi, src in enumerate((wa_ref, wx_ref)):
            for m in range(RG_HEADS // group):
                for a in range(group):
                    tiled = jnp.concatenate([src[m * group + a]] * group, axis=1)
                    wg_scr[gi * (RG_HEADS // group) + m, a * RG_HEAD_DIM:(a + 1) * RG_HEAD_DIM, :] = (
                        jnp.where(lane_head == a, tiled, 0.0).astype(BF16))
        project(normed(), range(n_chunks))

    @pl.when(jnp.logical_and(j > 0, j < nt))
    def _():
        to_tm, to_bm = permutations()
        hn = normed()
        rg_load(to_tm)
        project(hn, range(0, n_chunks // 2))
        rg_recur()
        project(hn, range(n_chunks // 2, n_chunks))
        rg_store(to_bm)

    @pl.when(j == nt)
    def _():
        to_tm, to_bm = permutations()
        rg_load(to_tm)
        rg_recur()
        rg_store(to_bm)


def _inproj_rglru(x, g_pre, w_in, conv_w, conv_b, w_rg_a, b_rg_a, w_rg_x, b_rg_x, lru_lambda, ts):
    nb, s, d = x.shape
    nt = s // ts
    row = lambda v: v.reshape(1, -1)
    const2 = lambda shape: pl.BlockSpec(shape, lambda j: (0, 0))
    const3 = lambda shape: pl.BlockSpec(shape, lambda j: (0, 0, 0))
    proj_idx = lambda j: (0, jnp.minimum(j, nt - 1), 0)
    y_idx = lambda j: (0, jnp.maximum(j - 1, 0), 0)
    w_out = D_IN - N_RG_CHUNKS * PROJ_CHUNK
    return pl.pallas_call(
        functools.partial(_inproj_rglru_kernel, nt=nt, nb=nb, ts=ts),
        grid=(nt + 1,),
        in_specs=[pl.BlockSpec((nb, ts, d), proj_idx),
                  const2((1, d)),
                  _resident((d, D_IN), lambda j: (0, 0)),
                  const2((CONV_WIDTH, RG_W)), const2((1, RG_W)),
                  const3(w_rg_a.shape), const2((1, RG_W)), const3(w_rg_x.shape), const2((1, RG_W)),
                  const2((1, RG_W))],
        out_specs=[pl.BlockSpec((nb, ts, w_out), proj_idx),
                   pl.BlockSpec((nb, ts, RG_W), y_idx)],
        out_shape=[jax.ShapeDtypeStruct((nb, s, w_out), BF16),
                   jax.ShapeDtypeStruct((nb, s, RG_W), BF16)],
        scratch_shapes=[
            pltpu.VMEM((2, nb, ts, N_RG_CHUNKS * PROJ_CHUNK), BF16),
            pltpu.VMEM((ts + CONV_WIDTH - 1, nb, RG_W), F32),
            pltpu.VMEM((ts, nb, RG_W), F32),
            pltpu.VMEM((ts, nb, RG_W), F32),
            pltpu.VMEM((nb, RG_W), F32),
            pltpu.VMEM((2 * RG_W // MXU_DIM, MXU_DIM, MXU_DIM), BF16),
        ],
        compiler_params=_params(("arbitrary",)),
        name="inproj_rglru",
    )(x, row(g_pre), w_in, conv_w, row(conv_b), w_rg_a, row(b_rg_a), w_rg_x, row(b_rg_x), row(lru_lambda))


def _attn_out_kernel(q_ref, k_ref, v_ref, g_ref, xq_ref, xg_ref, mem_ref, gm_ref, wm_ref,
                     lq1_ref, lk1_ref, lq2_ref, lk2_ref, gs_ref, yrg_ref, x_ref, wo_ref, gp_ref,
                     out_ref, vt_ref, km_ref, vmt_ref, y_scr, *, tq, nb):
    b = pl.program_id(0)
    h = pl.program_id(1)
    hd = DA_V_DIM
    dh = XA_HEAD_DIM
    s_len = q_ref.shape[0]
    n_mem = mem_ref.shape[0]
    nq = s_len // tq
    w = 2 * tq
    rows_o = out_ref.shape[0]
    cur = b & 1
    prev = (b + 1) & 1

    @pl.when(jnp.logical_and(h == 0, b < nb))
    def _():
        mn = _rms(mem_ref[...], gm_ref[...]).astype(BF16)
        kv = jnp.dot(mn, wm_ref[...].astype(BF16), preferred_element_type=F32)
        for hh in range(XA_HEADS):
            km_ref[hh] = kv[:, hh * dh:(hh + 1) * dh].astype(BF16)
            lo = XA_W + hh * dh
            vmt_ref[hh, 0:dh, :] = kv[:, lo:lo + dh].T.astype(BF16)
            vmt_ref[hh, dh:dh + BF16_ROWS, :] = jnp.ones((BF16_ROWS, n_mem), BF16)

    def softmax_t(s3, vt):
        n = s3.shape[0] * SUBLANES
        m = jnp.max(jnp.max(s3, axis=0), axis=0, keepdims=True)
        p = jnp.exp2(s3 - m[None]).reshape(n, s3.shape[2]).astype(BF16)
        return jnp.dot(vt, p, preferred_element_type=F32)

    def attention_chains():
        vt_ref[0:hd, :] = v_ref[...].astype(F32).T.astype(BF16)
        vt_ref[hd:hd + BF16_ROWS, :] = jnp.ones((BF16_ROWS, s_len), BF16)
        lam = (jnp.exp(jnp.sum(lq1_ref[...] * lk1_ref[...], axis=-1, keepdims=True))
               - jnp.exp(jnp.sum(lq2_ref[...] * lk2_ref[...], axis=-1, keepdims=True))
               + LAMBDA_INIT)
        lane = lax.broadcasted_iota(jnp.int32, (tq, hd), 1)
        key = lax.broadcasted_iota(jnp.int32, (tq, w), 0)
        qry = lax.broadcasted_iota(jnp.int32, (tq, w), 1) & (tq - 1)
        causal = key <= qry

        def da_scores(qb):
            kv = (qb + 1) * tq
            q = q_ref[qb * tq:kv, :].astype(F32)
            qq = jnp.concatenate([jnp.where(lane < DA_HEAD_DIM, q, 0.0),
                                  jnp.where(lane >= DA_HEAD_DIM, q, 0.0)], axis=0)
            qqt = qq.T.astype(BF16)
            s_diag = jnp.dot(k_ref[kv - tq:kv, :], qqt, preferred_element_type=F32)
            parts = [jnp.where(causal, s_diag, -jnp.inf)]
            if qb > 0:
                parts.insert(0, jnp.dot(k_ref[0:kv - tq, :], qqt, preferred_element_type=F32))
            return jnp.concatenate(parts, axis=0).reshape(kv // SUBLANES, SUBLANES, w)

        def da_finish(qb, s3):
            rows = slice(qb * tq, (qb + 1) * tq)
            pv = softmax_t(s3, vt_ref[:, 0:(qb + 1) * tq])
            ot = pv[0:hd] / pv[hd:hd + 1]
            od = (ot[:, 0:tq] - lam * ot[:, tq:w]).T
            o = _rms(od, gs_ref[...]) * (1.0 - LAMBDA_INIT)
            y_scr[cur, h, rows, :] = (o * _silu(g_ref[rows, :].astype(F32))).astype(BF16)

        def xa_scores(_):
            return _nt_dot(km_ref[h], xq_ref[...]).reshape(n_mem // SUBLANES, SUBLANES, s_len)

        def xa_finish(_, s3):
            pv = softmax_t(s3, vmt_ref[h])
            o = (pv[0:dh] / pv[dh:dh + 1]).T
            y_scr[cur, DA_HEADS + h] = (o * _silu(xg_ref[...].astype(F32))).astype(BF16)

        return ([(xa_scores, xa_finish, None)]
                + [(da_scores, da_finish, qb) for qb in range(nq - 1, -1, -1)])

    def op_products(_):
        r0 = pl.multiple_of(h * rows_o, rows_o)
        y_att = jnp.concatenate([y_scr[prev, c, pl.ds(r0, rows_o), :] for c in range(DA_HEADS + XA_HEADS)],
                                axis=1)
        y = jnp.dot(yrg_ref[...], wo_ref[0:RG_W, :].astype(BF16), preferred_element_type=F32)
        return y + jnp.dot(y_att, wo_ref[RG_W:, :].astype(BF16), preferred_element_type=F32)

    def op_finish(_, y):
        out_ref[...] = x_ref[...] + _rms(y, gp_ref[...])

    def run(chains):
        ahead = 2
        pending = [first(arg) for first, _, arg in chains[:ahead]]
        for pos, (_, finish, arg) in enumerate(chains):
            carried = pending.pop(0)
            if pos + ahead < len(chains):
                first, _, nxt = chains[pos + ahead]
                pending.append(first(nxt))
            finish(arg, carried)

    op_chain = [(op_products, op_finish, None)]

    @pl.when(b == 0)
    def _():
        run(attention_chains())

    @pl.when(jnp.logical_and(b > 0, b < nb))
    def _():
        att = attention_chains()
        run(att[:2] + op_chain + att[2:])

    @pl.when(b == nb)
    def _():
        run(op_chain)


def _attn_out(proj, y_rg, x, mem, g_mem, w_mem_kv, lq1, lk1, lq2, lk2, g_subln, w_out, g_post, tq):
    nb, s, d = x.shape
    m = mem.shape[1]
    hd = DA_V_DIM
    assert XA_HEAD_DIM == hd and XA_HEADS == DA_HEADS and s % DA_HEADS == 0
    per_chunk = PROJ_CHUNK // hd
    rows_o = s // DA_HEADS
    vec = lambda v: v.reshape(1, -1)
    cvec = lambda n: pl.BlockSpec((1, n), lambda bi, h: (0, 0))
    att_b = lambda bi: jnp.minimum(bi, nb - 1)
    att_h = lambda bi, h: jnp.where(bi < nb, h, DA_HEADS - 1)
    col = lambda chunk: pl.BlockSpec(
        (None, s, hd), lambda bi, h: (att_b(bi), 0, (chunk - N_RG_CHUNKS) * per_chunk + att_h(bi, h)))
    op_idx = lambda bi, h: (jnp.maximum(bi - 1, 0), jnp.where(bi > 0, h, 0), 0)
    return pl.pallas_call(
        functools.partial(_attn_out_kernel, tq=tq, nb=nb),
        grid=(nb + 1, DA_HEADS),
        in_specs=[col(C_DA_Q), col(C_DA_K), col(C_DA_V), col(C_DA_G), col(C_XA_Q), col(C_XA_G),
                  pl.BlockSpec((None, m, d), lambda bi, h: (att_b(bi), 0, 0)),
                  cvec(d),
                  _resident((d, 2 * XA_W), lambda bi, h: (0, 0)),
                  cvec(DA_HEAD_DIM), cvec(DA_HEAD_DIM), cvec(DA_HEAD_DIM), cvec(DA_HEAD_DIM),
                  cvec(DA_V_DIM),
                  pl.BlockSpec((None, rows_o, RG_W), op_idx),
                  pl.BlockSpec((None, rows_o, d), op_idx),
                  _resident(w_out.shape, lambda bi, h: (0, 0)),
                  cvec(d)],
        out_specs=pl.BlockSpec((None, rows_o, d), op_idx),
        out_shape=jax.ShapeDtypeStruct((nb, s, d), F32),
        scratch_shapes=[pltpu.VMEM((hd + BF16_ROWS, s), BF16),
                        pltpu.VMEM((XA_HEADS, m, hd), BF16),
                        pltpu.VMEM((XA_HEADS, hd + BF16_ROWS, m), BF16),
                        pltpu.VMEM((2, DA_HEADS + XA_HEADS, s, hd), BF16)],
        compiler_params=_params(("arbitrary", "arbitrary")),
        name="attn_out",
    )(proj, proj, proj, proj, proj, proj, mem, vec(g_mem), w_mem_kv,
      vec(lq1), vec(lk1), vec(lq2), vec(lk2), vec(g_subln), y_rg, x, w_out, vec(g_post))


def kernel(x, mem, g_pre, g_mem, w_in, w_mem_kv, conv_w, conv_b, w_rg_a, b_rg_a, w_rg_x, b_rg_x,
           lru_lambda, lambda_q1, lambda_k1, lambda_q2, lambda_k2, g_subln, w_out, g_post):
    proj, y_rg = _inproj_rglru(x, g_pre, w_in, conv_w, conv_b, w_rg_a, b_rg_a, w_rg_x, b_rg_x,
                               lru_lambda, ts=128)
    return _attn_out(proj, y_rg, x, mem, g_mem, w_mem_kv, lambda_q1, lambda_k1, lambda_q2, lambda_k2,
                     g_subln, w_out, g_post, tq=256)
```

```python
import functools
import math

import jax
import jax.numpy as jnp
from jax import lax
from jax.experimental import pallas as pl
from jax.experimental.pallas import tpu as pltpu

F32 = jnp.float32
BF16 = jnp.bfloat16

RMS_EPS = 1e-6
N_MEM = 256
RG_HEADS = 8
RG_HEAD_DIM = 64
RG_W = RG_HEADS * RG_HEAD_DIM
CONV_WIDTH = 4
LRU_C = 8.0
DA_HEADS = 4
DA_HEAD_DIM = 64
DA_V_DIM = 2 * DA_HEAD_DIM
DA_W = DA_HEADS * DA_V_DIM
LAMBDA_INIT = 0.8 - 0.6 * math.exp(-0.3 * (1 - 1))
XA_HEADS = 4
XA_HEAD_DIM = 128
XA_W = XA_HEADS * XA_HEAD_DIM
LOG2E = 1.4426950408889634

PROJ_CHUNK = 512
(C_RG_X, C_RG_G, C_DA_Q, C_DA_K, C_DA_V, C_DA_G, C_XA_Q, C_XA_G) = range(8)
D_IN = 8 * PROJ_CHUNK

SUBLANES = 8
LANES = 128
BF16_ROWS = 16
MXU_DIM = 256
VMEM_LIMIT_BYTES = 56 * 1024 * 1024


def _resident(shape, index_map):
    return pl.BlockSpec(shape, index_map, pipeline_mode=pl.Buffered(1))


def _params(sem):
    return pltpu.CompilerParams(dimension_semantics=sem, vmem_limit_bytes=VMEM_LIMIT_BYTES)


def _rms(x, g):
    ms = jnp.mean(x * x, axis=-1, keepdims=True)
    return x * lax.rsqrt(ms + RMS_EPS) * g


def _silu(x):
    h = 0.5 * x
    return h * (1.0 + jnp.tanh(h))


def _nt_dot(a, b):
    return lax.dot_general(a, b, (((1,), (1,)), ((), ())), preferred_element_type=F32)


RG_SUB = MXU_DIM // SUBLANES
N_RG_CHUNKS = 2


def _inproj_rglru_kernel(x_ref, g_ref, w_ref, cw_ref, cb_ref, wa_ref, ba_ref, wx_ref, bx_ref, lam_ref,
                         proj_ref, y_ref, rg_scr, xp_ref, a_ref, u_ref, h_ref, wg_scr, *, nt, nb, ts):
    j = pl.program_id(0)
    halo = CONV_WIDTH - 1
    sub = RG_SUB
    rows = sub * nb
    n_sub = ts // sub
    d = x_ref.shape[-1]
    assert rows == MXU_DIM and nb == SUBLANES and ts % sub == 0
    q_scale = {C_DA_Q: (DA_HEAD_DIM ** -0.5) * LOG2E, C_XA_Q: (XA_HEAD_DIM ** -0.5) * LOG2E}

    def normed():
        return _rms(x_ref[...].reshape(nb * ts, d), g_ref[...]).astype(BF16)

    def project(hn, chunks):
        for n in chunks:
            cols = slice(n * PROJ_CHUNK, (n + 1) * PROJ_CHUNK)
            acc = jnp.dot(hn, w_ref[:, cols].astype(BF16), preferred_element_type=F32)
            if n in q_scale:
                acc = acc * q_scale[n]
            acc = acc.astype(BF16).reshape(nb, ts, PROJ_CHUNK)
            if n < N_RG_CHUNKS:
                rg_scr[j & 1, :, :, cols] = acc
            else:
                per_chunk = PROJ_CHUNK // LANES
                for i in range(per_chunk):
                    proj_ref[:, (n - N_RG_CHUNKS) * per_chunk + i] = acc[:, :, i * LANES:(i + 1) * LANES]

    def permutations():
        r = lax.broadcasted_iota(jnp.int32, (rows, rows), 0)
        c = lax.broadcasted_iota(jnp.int32, (rows, rows), 1)
        log_nb, log_sub = nb.bit_length() - 1, sub.bit_length() - 1
        to_time_major = (c == (r & (nb - 1)) * sub + (r >> log_nb)).astype(BF16)
        to_batch_major = (c == (r & (sub - 1)) * nb + (r >> log_sub)).astype(BF16)
        return to_time_major, to_batch_major

    def rg_load(to_time_major):
        xp_ref[0:halo] = xp_ref[ts:ts + halo]
        prev = (j + 1) & 1
        for sb in range(n_sub):
            t0 = sb * sub
            blk = rg_scr[prev, :, t0:t0 + sub, 0:RG_W].reshape(rows, RG_W)
            tmaj = jnp.dot(to_time_major, blk, preferred_element_type=F32)
            xp_ref[halo + t0:halo + t0 + sub] = tmaj.reshape(sub, nb, RG_W)

    def rg_recur():
        xh = 0.5 * cb_ref[...].reshape(1, 1, RG_W)
        for k in range(CONV_WIDTH):
            xh = xh + (0.5 * cw_ref[k:k + 1, :]).reshape(1, 1, RG_W) * xp_ref[k:k + ts]
        xb = xh.reshape(ts * nb, RG_W).astype(BF16)

        n_blk = RG_W // MXU_DIM

        def gate(which, bg_ref):
            z = [jnp.dot(xb[:, m * MXU_DIM:(m + 1) * MXU_DIM], wg_scr[which * n_blk + m],
                         preferred_element_type=F32) for m in range(n_blk)]
            return jnp.tanh(jnp.concatenate(z, axis=1) + 0.5 * bg_ref[...]).reshape(ts, nb, RG_W)

        tr = gate(0, ba_ref)
        ti = gate(1, bx_ref)
        z = -lam_ref[...]
        softplus = jnp.maximum(z, 0.0) + jnp.log1p(jnp.exp(-jnp.abs(z)))
        half_rate = ((-0.5 * LRU_C * LOG2E) * softplus).reshape(1, 1, RG_W)
        a = jnp.exp2(tr * half_rate + half_rate)
        d = jnp.maximum(1.0 - a * a, 0.0)
        mult = jnp.where(d > 0.0, d * lax.rsqrt(d), 0.0)
        a_ref[...] = a
        u_ref[...] = mult * (xh * (1.0 + ti))
        h = h_ref[...]
        for t in range(ts):
            h = a_ref[t] * h + u_ref[t]
            u_ref[t] = h
        h_ref[...] = h

    def rg_store(to_batch_major):
        prev = (j + 1) & 1
        for sb in range(n_sub):
            t0 = sb * sub
            h_tm = u_ref[t0:t0 + sub].reshape(rows, RG_W).astype(BF16)
            h_bm = jnp.dot(to_batch_major, h_tm, preferred_element_type=F32).reshape(nb, sub, RG_W)
            g_bm = rg_scr[prev, :, t0:t0 + sub, RG_W:2 * RG_W].astype(F32)
            y_ref[:, t0:t0 + sub, :] = (h_bm * _silu(g_bm)).astype(BF16)

    n_chunks = D_IN // PROJ_CHUNK

    @pl.when(j == 0)
    def _():
        xp_ref[...] = jnp.zeros(xp_ref.shape, F32)
        h_ref[...] = jnp.zeros(h_ref.shape, F32)
        group = MXU_DIM // RG_HEAD_DIM
        lane_head = lax.broadcasted_iota(jnp.int32, (RG_HEAD_DIM, MXU_DIM), 1) // RG_HEAD_DIM
        for gi, src in enumerate((wa_ref, wx_ref)):
            for m in range(RG_HEADS // group):
                for a in range(group):
                    tiled = jnp.concatenate([src[m * group + a]] * group, axis=1)
                    wg_scr[gi * (RG_HEADS // group) + m, a * RG_HEAD_DIM:(a + 1) * RG_HEAD_DIM, :] = (
                        jnp.where(lane_head == a, tiled, 0.0).astype(BF16))
        project(normed(), range(n_chunks))

    @pl.when(jnp.logical_and(j > 0, j < nt))
    def _():
        to_tm, to_bm = permutations()
        hn = normed()
        rg_load(to_tm)
        project(hn, range(0, n_chunks // 2))
        rg_recur()
        project(hn, range(n_chunks // 2, n_chunks))
        rg_store(to_bm)

    @pl.when(j == nt)
    def _():
        to_tm, to_bm = permutations()
        rg_load(to_tm)
        rg_recur()
        rg_store(to_bm)


def _inproj_rglru(x, g_pre, w_in, conv_w, conv_b, w_rg_a, b_rg_a, w_rg_x, b_rg_x, lru_lambda, ts):
    nb, s, d = x.shape
    nt = s // ts
    row = lambda v: v.reshape(1, -1)
    const2 = lambda shape: pl.BlockSpec(shape, lambda j: (0, 0))
    const3 = lambda shape: pl.BlockSpec(shape, lambda j: (0, 0, 0))
    proj_idx = lambda j: (0, jnp.minimum(j, nt - 1), 0)
    y_idx = lambda j: (0, jnp.maximum(j - 1, 0), 0)
    n_slabs = (D_IN - N_RG_CHUNKS * PROJ_CHUNK) // LANES
    return pl.pallas_call(
        functools.partial(_inproj_rglru_kernel, nt=nt, nb=nb, ts=ts),
        grid=(nt + 1,),
        in_specs=[pl.BlockSpec((nb, ts, d), proj_idx),
                  const2((1, d)),
                  _resident((d, D_IN), lambda j: (0, 0)),
                  const2((CONV_WIDTH, RG_W)), const2((1, RG_W)),
                  const3(w_rg_a.shape), const2((1, RG_W)), const3(w_rg_x.shape), const2((1, RG_W)),
                  const2((1, RG_W))],
        out_specs=[pl.BlockSpec((nb, n_slabs, ts, LANES), lambda j: (0, 0, jnp.minimum(j, nt - 1), 0)),
                   pl.BlockSpec((nb, ts, RG_W), y_idx)],
        out_shape=[jax.ShapeDtypeStruct((nb, n_slabs, s, LANES), BF16),
                   jax.ShapeDtypeStruct((nb, s, RG_W), BF16)],
        scratch_shapes=[
            pltpu.VMEM((2, nb, ts, N_RG_CHUNKS * PROJ_CHUNK), BF16),
            pltpu.VMEM((ts + CONV_WIDTH - 1, nb, RG_W), F32),
            pltpu.VMEM((ts, nb, RG_W), F32),
            pltpu.VMEM((ts, nb, RG_W), F32),
            pltpu.VMEM((nb, RG_W), F32),
            pltpu.VMEM((2 * RG_W // MXU_DIM, MXU_DIM, MXU_DIM), BF16),
        ],
        compiler_params=_params(("arbitrary",)),
        name="inproj_rglru",
    )(x, row(g_pre), w_in, conv_w, row(conv_b), w_rg_a, row(b_rg_a), w_rg_x, row(b_rg_x), row(lru_lambda))


def _attn_out_kernel(q_ref, k_ref, v_ref, g_ref, xq_ref, xg_ref, mem_ref, gm_ref, wm_ref,
                     lq1_ref, lk1_ref, lq2_ref, lk2_ref, gs_ref, yrg_ref, x_ref, wo_ref, gp_ref,
                     out_ref, vt_ref, km_ref, vmt_ref, y_scr, *, tq, nb):
    b = pl.program_id(0)
    h = pl.program_id(1)
    hd = DA_V_DIM
    dh = XA_HEAD_DIM
    s_len = q_ref.shape[0]
    n_mem = mem_ref.shape[0]
    nq = s_len // tq
    w = 2 * tq
    rows_o = out_ref.shape[0]
    cur = b & 1
    prev = (b + 1) & 1

    @pl.when(jnp.logical_and(h == 0, b < nb))
    def _():
        mn = _rms(mem_ref[...], gm_ref[...]).astype(BF16)
        kv = jnp.dot(mn, wm_ref[...].astype(BF16), preferred_element_type=F32)
        for hh in range(XA_HEADS):
            km_ref[hh] = kv[:, hh * dh:(hh + 1) * dh].astype(BF16)
            lo = XA_W + hh * dh
            vmt_ref[hh, 0:dh, :] = kv[:, lo:lo + dh].T.astype(BF16)
            vmt_ref[hh, dh:dh + BF16_ROWS, :] = jnp.ones((BF16_ROWS, n_mem), BF16)

    def softmax_t(s3, vt):
        n = s3.shape[0] * SUBLANES
        m = jnp.max(jnp.max(s3, axis=0), axis=0, keepdims=True)
        p = jnp.exp2(s3 - m[None]).reshape(n, s3.shape[2]).astype(BF16)
        return jnp.dot(vt, p, preferred_element_type=F32)

    def attention_chains():
        vt_ref[0:hd, :] = v_ref[...].astype(F32).T.astype(BF16)
        vt_ref[hd:hd + BF16_ROWS, :] = jnp.ones((BF16_ROWS, s_len), BF16)
        lam = (jnp.exp(jnp.sum(lq1_ref[...] * lk1_ref[...], axis=-1, keepdims=True))
               - jnp.exp(jnp.sum(lq2_ref[...] * lk2_ref[...], axis=-1, keepdims=True))
               + LAMBDA_INIT)
        lane = lax.broadcasted_iota(jnp.int32, (tq, hd), 1)
        key = lax.broadcasted_iota(jnp.int32, (tq, w), 0)
        qry = lax.broadcasted_iota(jnp.int32, (tq, w), 1) & (tq - 1)
        causal = key <= qry

        def da_scores(qb):
            kv = (qb + 1) * tq
            q = q_ref[qb * tq:kv, :].astype(F32)
            qq = jnp.concatenate([jnp.where(lane < DA_HEAD_DIM, q, 0.0),
                                  jnp.where(lane >= DA_HEAD_DIM, q, 0.0)], axis=0)
            qqt = qq.T.astype(BF16)
            s_diag = jnp.dot(k_ref[kv - tq:kv, :], qqt, preferred_element_type=F32)
            parts = [jnp.where(causal, s_diag, -jnp.inf)]
            if qb > 0:
                parts.insert(0, jnp.dot(k_ref[0:kv - tq, :], qqt, preferred_element_type=F32))
            return jnp.concatenate(parts, axis=0).reshape(kv // SUBLANES, SUBLANES, w)

        def da_finish(qb, s3):
            rows = slice(qb * tq, (qb + 1) * tq)
            pv = softmax_t(s3, vt_ref[:, 0:(qb + 1) * tq])
            ot = pv[0:hd] / pv[hd:hd + 1]
            od = (ot[:, 0:tq] - lam * ot[:, tq:w]).T
            o = _rms(od, gs_ref[...]) * (1.0 - LAMBDA_INIT)
            y_scr[cur, h, rows, :] = (o * _silu(g_ref[rows, :].astype(F32))).astype(BF16)

        def xa_scores(_):
            return _nt_dot(km_ref[h], xq_ref[...]).reshape(n_mem // SUBLANES, SUBLANES, s_len)

        def xa_finish(_, s3):
            pv = softmax_t(s3, vmt_ref[h])
            o = (pv[0:dh] / pv[dh:dh + 1]).T
            y_scr[cur, DA_HEADS + h] = (o * _silu(xg_ref[...].astype(F32))).astype(BF16)

        return ([(xa_scores, xa_finish, None)]
                + [(da_scores, da_finish, qb) for qb in range(nq - 1, -1, -1)])

    def op_products(_):
        r0 = pl.multiple_of(h * rows_o, rows_o)
        y_att = jnp.concatenate([y_scr[prev, c, pl.ds(r0, rows_o), :] for c in range(DA_HEADS + XA_HEADS)],
                                axis=1)
        y = jnp.dot(yrg_ref[...], wo_ref[0:RG_W, :].astype(BF16), preferred_element_type=F32)
        return y + jnp.dot(y_att, wo_ref[RG_W:, :].astype(BF16), preferred_element_type=F32)

    def op_finish(_, y):
        out_ref[...] = x_ref[...] + _rms(y, gp_ref[...])

    def run(chains):
        ahead = 2
        pending = [first(arg) for first, _, arg in chains[:ahead]]
        for pos, (_, finish, arg) in enumerate(chains):
            carried = pending.pop(0)
            if pos + ahead < len(chains):
                first, _, nxt = chains[pos + ahead]
                pending.append(first(nxt))
            finish(arg, carried)

    op_chain = [(op_products, op_finish, None)]

    @pl.when(b == 0)
    def _():
        run(attention_chains())

    @pl.when(jnp.logical_and(b > 0, b < nb))
    def _():
        att = attention_chains()
        run(att[:2] + op_chain + att[2:])

    @pl.when(b == nb)
    def _():
        run(op_chain)


def _attn_out(proj, y_rg, x, mem, g_mem, w_mem_kv, lq1, lk1, lq2, lk2, g_subln, w_out, g_post, tq):
    nb, s, d = x.shape
    m = mem.shape[1]
    hd = DA_V_DIM
    assert XA_HEAD_DIM == hd and XA_HEADS == DA_HEADS and s % DA_HEADS == 0
    per_chunk = PROJ_CHUNK // hd
    rows_o = s // DA_HEADS
    vec = lambda v: v.reshape(1, -1)
    cvec = lambda n: pl.BlockSpec((1, n), lambda bi, h: (0, 0))
    att_b = lambda bi: jnp.minimum(bi, nb - 1)
    att_h = lambda bi, h: jnp.where(bi < nb, h, DA_HEADS - 1)
    col = lambda chunk: pl.BlockSpec(
        (None, None, s, hd), lambda bi, h: (att_b(bi), (chunk - N_RG_CHUNKS) * per_chunk + att_h(bi, h), 0, 0))
    op_idx = lambda bi, h: (jnp.maximum(bi - 1, 0), jnp.where(bi > 0, h, 0), 0)
    return pl.pallas_call(
        functools.partial(_attn_out_kernel, tq=tq, nb=nb),
        grid=(nb + 1, DA_HEADS),
        in_specs=[col(C_DA_Q), col(C_DA_K), col(C_DA_V), col(C_DA_G), col(C_XA_Q), col(C_XA_G),
                  pl.BlockSpec((None, m, d), lambda bi, h: (att_b(bi), 0, 0)),
                  cvec(d),
                  _resident((d, 2 * XA_W), lambda bi, h: (0, 0)),
                  cvec(DA_HEAD_DIM), cvec(DA_HEAD_DIM), cvec(DA_HEAD_DIM), cvec(DA_HEAD_DIM),
                  cvec(DA_V_DIM),
                  pl.BlockSpec((None, rows_o, RG_W), op_idx),
                  pl.BlockSpec((None, rows_o, d), op_idx),
                  _resident(w_out.shape, lambda bi, h: (0, 0)),
                  cvec(d)],
        out_specs=pl.BlockSpec((None, rows_o, d), op_idx),
        out_shape=jax.ShapeDtypeStruct((nb, s, d), F32),
        scratch_shapes=[pltpu.VMEM((hd + BF16_ROWS, s), BF16),
                        pltpu.VMEM((XA_HEADS, m, hd), BF16),
                        pltpu.VMEM((XA_HEADS, hd + BF16_ROWS, m), BF16),
                        pltpu.VMEM((2, DA_HEADS + XA_HEADS, s, hd), BF16)],
        compiler_params=_params(("arbitrary", "arbitrary")),
        name="attn_out",
    )(proj, proj, proj, proj, proj, proj, mem, vec(g_mem), w_mem_kv,
      vec(lq1), vec(lk1), vec(lq2), vec(lk2), vec(g_subln), y_rg, x, w_out, vec(g_post))


def kernel(x, mem, g_pre, g_mem, w_in, w_mem_kv, conv_w, conv_b, w_rg_a, b_rg_a, w_rg_x, b_rg_x,
           lru_lambda, lambda_q1, lambda_k1, lambda_q2, lambda_k2, g_subln, w_out, g_post):
    proj, y_rg = _inproj_rglru(x, g_pre, w_in, conv_w, conv_b, w_rg_a, b_rg_a, w_rg_x, b_rg_x,
                               lru_lambda, ts=128)
    return _attn_out(proj, y_rg, x, mem, g_mem, w_mem_kv, lambda_q1, lambda_k1, lambda_q2, lambda_k2,
                     g_subln, w_out, g_post, tq=256)
```

```python
import functools
import math

import jax
import jax.numpy as jnp
from jax import lax
from jax.experimental import pallas as pl
from jax.experimental.pallas import tpu as pltpu

F32 = jnp.float32
BF16 = jnp.bfloat16

RMS_EPS = 1e-6
RG_HEADS = 8
RG_HEAD_DIM = 64
RG_W = RG_HEADS * RG_HEAD_DIM
CONV_WIDTH = 4
LRU_C = 8.0
DA_HEADS = 4
DA_HEAD_DIM = 64
DA_V_DIM = 2 * DA_HEAD_DIM
DA_W = DA_HEADS * DA_V_DIM
LAMBDA_INIT = 0.8 - 0.6 * math.exp(-0.3 * (1 - 1))
XA_HEADS = 4
XA_HEAD_DIM = 128
XA_W = XA_HEADS * XA_HEAD_DIM
LOG2E = 1.4426950408889634

PROJ_CHUNK = 512
(C_RG_X, C_RG_G, C_DA_Q, C_DA_K, C_DA_V, C_DA_G, C_XA_Q, C_XA_G) = range(8)
D_IN = 8 * PROJ_CHUNK

SUBLANES = 8
LANES = 128
BF16_ROWS = 16
MXU_DIM = 256
VMEM_LIMIT_BYTES = 56 * 1024 * 1024


def _resident(shape, index_map):
    return pl.BlockSpec(shape, index_map, pipeline_mode=pl.Buffered(1))


def _params(sem):
    return pltpu.CompilerParams(dimension_semantics=sem, vmem_limit_bytes=VMEM_LIMIT_BYTES)


def _rms(x, g):
    ms = jnp.mean(x * x, axis=-1, keepdims=True)
    return x * lax.rsqrt(ms + RMS_EPS) * g


def _silu(x):
    h = 0.5 * x
    return h * (1.0 + jnp.tanh(h))


def _nt_dot(a, b):
    return lax.dot_general(a, b, (((1,), (1,)), ((), ())), preferred_element_type=F32)


RG_SUB = MXU_DIM // SUBLANES
N_RG_CHUNKS = 2


def _inproj_rglru_kernel(x_ref, g_ref, w_ref, cw_ref, cb_ref, wa_ref, ba_ref, wx_ref, bx_ref, lam_ref,
                         proj_ref, y_ref, rg_scr, xp_ref, a_ref, u_ref, h_ref, wg_scr, *, nt, nb, ts):
    j = pl.program_id(0)
    halo = CONV_WIDTH - 1
    sub = RG_SUB
    rows = sub * nb
    n_sub = ts // sub
    d = x_ref.shape[-1]
    assert rows == MXU_DIM and nb == SUBLANES and ts % sub == 0
    q_scale = {C_DA_Q: (DA_HEAD_DIM ** -0.5) * LOG2E, C_XA_Q: (XA_HEAD_DIM ** -0.5) * LOG2E}

    def normed():
        return _rms(x_ref[...].reshape(nb * ts, d), g_ref[...]).astype(BF16)

    def project(hn, chunks):
        for n in chunks:
            cols = slice(n * PROJ_CHUNK, (n + 1) * PROJ_CHUNK)
            acc = jnp.dot(hn, w_ref[:, cols].astype(BF16), preferred_element_type=F32)
            if n in q_scale:
                acc = acc * q_scale[n]
            acc = acc.astype(BF16).reshape(nb, ts, PROJ_CHUNK)
            if n < N_RG_CHUNKS:
                rg_scr[j & 1, :, :, cols] = acc
            else:
                per_chunk = PROJ_CHUNK // LANES
                for i in range(per_chunk):
                    proj_ref[:, (n - N_RG_CHUNKS) * per_chunk + i] = acc[:, :, i * LANES:(i + 1) * LANES]

    def permutations():
        r = lax.broadcasted_iota(jnp.int32, (rows, rows), 0)
        c = lax.broadcasted_iota(jnp.int32, (rows, rows), 1)
        log_nb, log_sub = nb.bit_length() - 1, sub.bit_length() - 1
        to_time_major = (c == (r & (nb - 1)) * sub + (r >> log_nb)).astype(BF16)
        to_batch_major = (c == (r & (sub - 1)) * nb + (r >> log_sub)).astype(BF16)
        return to_time_major, to_batch_major

    def rg_load(to_time_major):
        xp_ref[0:halo] = xp_ref[ts:ts + halo]
        prev = (j + 1) & 1
        for sb in range(n_sub):
            t0 = sb * sub
            blk = rg_scr[prev, :, t0:t0 + sub, 0:RG_W].reshape(rows, RG_W)
            tmaj = jnp.dot(to_time_major, blk, preferred_element_type=F32)
            xp_ref[halo + t0:halo + t0 + sub] = tmaj.reshape(sub, nb, RG_W)

    def rg_recur():
        xh = 0.5 * cb_ref[...].reshape(1, 1, RG_W)
        for k in range(CONV_WIDTH):
            xh = xh + (0.5 * cw_ref[k:k + 1, :]).reshape(1, 1, RG_W) * xp_ref[k:k + ts]
        xb = xh.reshape(ts * nb, RG_W).astype(BF16)

        n_blk = RG_W // MXU_DIM

        def gate(which, bg_ref):
            z = [jnp.dot(xb[:, m * MXU_DIM:(m + 1) * MXU_DIM], wg_scr[which * n_blk + m],
                         preferred_element_type=F32) for m in range(n_blk)]
            return jnp.tanh(jnp.concatenate(z, axis=1) + 0.5 * bg_ref[...]).reshape(ts, nb, RG_W)

        tr = gate(0, ba_ref)
        ti = gate(1, bx_ref)
        z = -lam_ref[...]
        softplus = jnp.maximum(z, 0.0) + jnp.log1p(jnp.exp(-jnp.abs(z)))
        half_rate = ((-0.5 * LRU_C * LOG2E) * softplus).reshape(1, 1, RG_W)
        a = jnp.exp2(tr * half_rate + half_rate)
        d = jnp.maximum(1.0 - a * a, 0.0)
        mult = jnp.where(d > 0.0, d * lax.rsqrt(d), 0.0)
        a_ref[...] = a
        u_ref[...] = mult * (xh * (1.0 + ti))
        h = h_ref[...]
        for t in range(ts):
            h = a_ref[t] * h + u_ref[t]
            u_ref[t] = h
        h_ref[...] = h

    def rg_store(to_batch_major):
        prev = (j + 1) & 1
        for sb in range(n_sub):
            t0 = sb * sub
            h_tm = u_ref[t0:t0 + sub].reshape(rows, RG_W).astype(BF16)
            h_bm = jnp.dot(to_batch_major, h_tm, preferred_element_type=F32).reshape(nb, sub, RG_W)
            g_bm = rg_scr[prev, :, t0:t0 + sub, RG_W:2 * RG_W].astype(F32)
            y_ref[:, t0:t0 + sub, :] = (h_bm * _silu(g_bm)).astype(BF16)

    n_chunks = D_IN // PROJ_CHUNK

    @pl.when(j == 0)
    def _():
        xp_ref[...] = jnp.zeros(xp_ref.shape, F32)
        h_ref[...] = jnp.zeros(h_ref.shape, F32)
        group = MXU_DIM // RG_HEAD_DIM
        lane_head = lax.broadcasted_iota(jnp.int32, (RG_HEAD_DIM, MXU_DIM), 1) // RG_HEAD_DIM
        for gi, src in enumerate((wa_ref, wx_ref)):
            for m in range(RG_HEADS // group):
                for a in range(group):
                    tiled = jnp.concatenate([src[m * group + a]] * group, axis=1)
                    wg_scr[gi * (RG_HEADS // group) + m, a * RG_HEAD_DIM:(a + 1) * RG_HEAD_DIM, :] = (
                        jnp.where(lane_head == a, tiled, 0.0).astype(BF16))
        project(normed(), range(n_chunks))

    @pl.when(jnp.logical_and(j > 0, j < nt))
    def _():
        to_tm, to_bm = permutations()
        hn = normed()
        rg_load(to_tm)
        project(hn, range(0, n_chunks // 2))
        rg_recur()
        project(hn, range(n_chunks // 2, n_chunks))
        rg_store(to_bm)

    @pl.when(j == nt)
    def _():
        to_tm, to_bm = permutations()
        rg_load(to_tm)
        rg_recur()
        rg_store(to_bm)


def _inproj_rglru(x, g_pre, w_in, conv_w, conv_b, w_rg_a, b_rg_a, w_rg_x, b_rg_x, lru_lambda, ts):
    nb, s, d = x.shape
    nt = s // ts
    row = lambda v: v.reshape(1, -1)
    const2 = lambda shape: pl.BlockSpec(shape, lambda j: (0, 0))
    const3 = lambda shape: pl.BlockSpec(shape, lambda j: (0, 0, 0))
    proj_idx = lambda j: (0, jnp.minimum(j, nt - 1), 0)
    y_idx = lambda j: (0, jnp.maximum(j - 1, 0), 0)
    n_slabs = (D_IN - N_RG_CHUNKS * PROJ_CHUNK) // LANES
    return pl.pallas_call(
        functools.partial(_inproj_rglru_kernel, nt=nt, nb=nb, ts=ts),
        grid=(nt + 1,),
        in_specs=[pl.BlockSpec((nb, ts, d), proj_idx),
                  const2((1, d)),
                  _resident((d, D_IN), lambda j: (0, 0)),
                  const2((CONV_WIDTH, RG_W)), const2((1, RG_W)),
                  const3(w_rg_a.shape), const2((1, RG_W)), const3(w_rg_x.shape), const2((1, RG_W)),
                  const2((1, RG_W))],
        out_specs=[pl.BlockSpec((nb, n_slabs, ts, LANES), lambda j: (0, 0, jnp.minimum(j, nt - 1), 0)),
                   pl.BlockSpec((nb, ts, RG_W), y_idx)],
        out_shape=[jax.ShapeDtypeStruct((nb, n_slabs, s, LANES), BF16),
                   jax.ShapeDtypeStruct((nb, s, RG_W), BF16)],
        scratch_shapes=[
            pltpu.VMEM((2, nb, ts, N_RG_CHUNKS * PROJ_CHUNK), BF16),
            pltpu.VMEM((ts + CONV_WIDTH - 1, nb, RG_W), F32),
            pltpu.VMEM((ts, nb, RG_W), F32),
            pltpu.VMEM((ts, nb, RG_W), F32),
            pltpu.VMEM((nb, RG_W), F32),
            pltpu.VMEM((2 * RG_W // MXU_DIM, MXU_DIM, MXU_DIM), BF16),
        ],
        compiler_params=_params(("arbitrary",)),
        name="inproj_rglru",
    )(x, row(g_pre), w_in, conv_w, row(conv_b), w_rg_a, row(b_rg_a), w_rg_x, row(b_rg_x), row(lru_lambda))


def _attn_out_kernel(q_ref, k_ref, v_ref, g_ref, xq_ref, xg_ref, mem_ref, gm_ref, wm_ref,
                     lq1_ref, lk1_ref, lq2_ref, lk2_ref, gs_ref, yrg_ref, x_ref, wo_ref, gp_ref,
                     out_ref, vt_ref, km_ref, vmt_ref, y_scr, *, tq, nb):
    b = pl.program_id(0)
    h = pl.program_id(1)
    hd = DA_V_DIM
    dh = XA_HEAD_DIM
    s_len = q_ref.shape[0]
    n_mem = mem_ref.shape[0]
    nq = s_len // tq
    w = 2 * tq
    rows_o = out_ref.shape[0]
    cur = b & 1
    prev = (b + 1) & 1

    @pl.when(jnp.logical_and(h == 0, b < nb))
    def _():
        mn = _rms(mem_ref[...], gm_ref[...]).astype(BF16)
        kv = jnp.dot(mn, wm_ref[...].astype(BF16), preferred_element_type=F32)
        for hh in range(XA_HEADS):
            km_ref[hh] = kv[:, hh * dh:(hh + 1) * dh].astype(BF16)
            lo = XA_W + hh * dh
            vmt_ref[hh, 0:dh, :] = kv[:, lo:lo + dh].T.astype(BF16)
            vmt_ref[hh, dh:dh + BF16_ROWS, :] = jnp.ones((BF16_ROWS, n_mem), BF16)

    def softmax_t(s3, vt):
        n = s3.shape[0] * SUBLANES
        m = jnp.max(jnp.max(s3, axis=0), axis=0, keepdims=True)
        p = jnp.exp2(s3 - m[None]).reshape(n, s3.shape[2]).astype(BF16)
        return jnp.dot(vt, p, preferred_element_type=F32)

    def attention_chains():
        vt_ref[0:hd, :] = v_ref[...].astype(F32).T.astype(BF16)
        vt_ref[hd:hd + BF16_ROWS, :] = jnp.ones((BF16_ROWS, s_len), BF16)
        lam = (jnp.exp(jnp.sum(lq1_ref[...] * lk1_ref[...], axis=-1, keepdims=True))
               - jnp.exp(jnp.sum(lq2_ref[...] * lk2_ref[...], axis=-1, keepdims=True))
               + LAMBDA_INIT)
        lane = lax.broadcasted_iota(jnp.int32, (tq, hd), 1)
        key = lax.broadcasted_iota(jnp.int32, (tq, w), 0)
        qry = lax.broadcasted_iota(jnp.int32, (tq, w), 1) & (tq - 1)
        causal = key <= qry

        def da_scores(qb):
            kv = (qb + 1) * tq
            q = q_ref[qb * tq:kv, :].astype(F32)
            qq = jnp.concatenate([jnp.where(lane < DA_HEAD_DIM, q, 0.0),
                                  jnp.where(lane >= DA_HEAD_DIM, q, 0.0)], axis=0)
            qqt = qq.T.astype(BF16)
            s_diag = jnp.dot(k_ref[kv - tq:kv, :], qqt, preferred_element_type=F32)
            parts = [jnp.where(causal, s_diag, -jnp.inf)]
            if qb > 0:
                parts.insert(0, jnp.dot(k_ref[0:kv - tq, :], qqt, preferred_element_type=F32))
            return jnp.concatenate(parts, axis=0).reshape(kv // SUBLANES, SUBLANES, w)

        def da_finish(qb, s3):
            rows = slice(qb * tq, (qb + 1) * tq)
            pv = softmax_t(s3, vt_ref[:, 0:(qb + 1) * tq])
            ot = pv[0:hd] / pv[hd:hd + 1]
            od = (ot[:, 0:tq] - lam * ot[:, tq:w]).T
            o = _rms(od, gs_ref[...]) * (1.0 - LAMBDA_INIT)
            y_scr[cur, h, rows, :] = (o * _silu(g_ref[rows, :].astype(F32))).astype(BF16)

        def xa_scores(_):
            return _nt_dot(km_ref[h], xq_ref[...]).reshape(n_mem // SUBLANES, SUBLANES, s_len)

        def xa_finish(_, s3):
            pv = softmax_t(s3, vmt_ref[h])
            o = (pv[0:dh] / pv[dh:dh + 1]).T
            y_scr[cur, DA_HEADS + h] = (o * _silu(xg_ref[...].astype(F32))).astype(BF16)

        return ([(xa_scores, xa_finish, None)]
                + [(da_scores, da_finish, qb) for qb in range(nq - 1, -1, -1)])

    def op_products(_):
        r0 = pl.multiple_of(h * rows_o, rows_o)
        y_att = jnp.concatenate([y_scr[prev, c, pl.ds(r0, rows_o), :] for c in range(DA_HEADS + XA_HEADS)],
                                axis=1)
        y = jnp.dot(yrg_ref[...], wo_ref[0:RG_W, :].astype(BF16), preferred_element_type=F32)
        return y + jnp.dot(y_att, wo_ref[RG_W:, :].astype(BF16), preferred_element_type=F32)

    def op_finish(_, y):
        out_ref[...] = x_ref[...] + _rms(y, gp_ref[...])

    def run(chains):
        ahead = 2
        pending = [first(arg) for first, _, arg in chains[:ahead]]
        for pos, (_, finish, arg) in enumerate(chains):
            carried = pending.pop(0)
            if pos + ahead < len(chains):
                first, _, nxt = chains[pos + ahead]
                pending.append(first(nxt))
            finish(arg, carried)

    op_chain = [(op_products, op_finish, None)]

    @pl.when(b == 0)
    def _():
        run(attention_chains())

    @pl.when(jnp.logical_and(b > 0, b < nb))
    def _():
        att = attention_chains()
        run(att[:2] + op_chain + att[2:])

    @pl.when(b == nb)
    def _():
        run(op_chain)


def _attn_out(proj, y_rg, x, mem, g_mem, w_mem_kv, lq1, lk1, lq2, lk2, g_subln, w_out, g_post, tq):
    nb, s, d = x.shape
    m = mem.shape[1]
    hd = DA_V_DIM
    assert XA_HEAD_DIM == hd and XA_HEADS == DA_HEADS and s % DA_HEADS == 0
    per_chunk = PROJ_CHUNK // hd
    rows_o = s // DA_HEADS
    vec = lambda v: v.reshape(1, -1)
    cvec = lambda n: pl.BlockSpec((1, n), lambda bi, h: (0, 0))
    att_b = lambda bi: jnp.minimum(bi, nb - 1)
    att_h = lambda bi, h: jnp.where(bi < nb, h, DA_HEADS - 1)
    col = lambda chunk: pl.BlockSpec(
        (None, None, s, hd), lambda bi, h: (att_b(bi), (chunk - N_RG_CHUNKS) * per_chunk + att_h(bi, h), 0, 0))
    op_idx = lambda bi, h: (jnp.maximum(bi - 1, 0), jnp.where(bi > 0, h, 0), 0)
    return pl.pallas_call(
        functools.partial(_attn_out_kernel, tq=tq, nb=nb),
        grid=(nb + 1, DA_HEADS),
        in_specs=[col(C_DA_Q), col(C_DA_K), col(C_DA_V), col(C_DA_G), col(C_XA_Q), col(C_XA_G),
                  pl.BlockSpec((None, m, d), lambda bi, h: (att_b(bi), 0, 0)),
                  cvec(d),
                  _resident((d, 2 * XA_W), lambda bi, h: (0, 0)),
                  cvec(DA_HEAD_DIM), cvec(DA_HEAD_DIM), cvec(DA_HEAD_DIM), cvec(DA_HEAD_DIM),
                  cvec(DA_V_DIM),
                  pl.BlockSpec((None, rows_o, RG_W), op_idx),
                  pl.BlockSpec((None, rows_o, d), op_idx),
                  _resident(w_out.shape, lambda bi, h: (0, 0)),
                  cvec(d)],
        out_specs=pl.BlockSpec((None, rows_o, d), op_idx),
        out_shape=jax.ShapeDtypeStruct((nb, s, d), F32),
        scratch_shapes=[pltpu.VMEM((hd + BF16_ROWS, s), BF16),
                        pltpu.VMEM((XA_HEADS, m, hd), BF16),
                        pltpu.VMEM((XA_HEADS, hd + BF16_ROWS, m), BF16),
                        pltpu.VMEM((2, DA_HEADS + XA_HEADS, s, hd), BF16)],
        compiler_params=_params(("arbitrary", "arbitrary")),
        name="attn_out",
    )(proj, proj, proj, proj, proj, proj, mem, vec(g_mem), w_mem_kv,
      vec(lq1), vec(lk1), vec(lq2), vec(lk2), vec(g_subln), y_rg, x, w_out, vec(g_post))


def kernel(x, mem, g_pre, g_mem, w_in, w_mem_kv, conv_w, conv_b, w_rg_a, b_rg_a, w_rg_x, b_rg_x,
           lru_lambda, lambda_q1, lambda_k1, lambda_q2, lambda_k2, g_subln, w_out, g_post):
    proj, y_rg = _inproj_rglru(x, g_pre, w_in, conv_w, conv_b, w_rg_a, b_rg_a, w_rg_x, b_rg_x,
                               lru_lambda, ts=128)
    return _attn_out(proj, y_rg, x, mem, g_mem, w_mem_kv, lambda_q1, lambda_k1, lambda_q2, lambda_k2,
                     g_subln, w_out, g_post, tq=256)
```
